```python
import jax
import jax.numpy as jnp
from jax import lax
import numpy as np


D_MODEL = 1024
BATCH = 8
SEQ = 8192
DEPTH = 4

MEM_LEN = 256
EPS = 1e-6
N_EVEN = (DEPTH + 1) // 2
N_ODD = DEPTH // 2

MLSTM_HEADS = 4
MLSTM_DQK = 64
MLSTM_DV = 128
MLSTM_CHUNK = 64
GATE_SOFTCAP = 15.0

SB_HEADS = 4
SB_DH = 128
SB_BLOCK = 128

GDN_HEADS = 8
GDN_DK = 128
GDN_DV = 128
GDN_CONV = 4
GDN_CHUNK = 64

XA_HEADS = 4
XA_DH = D_MODEL // XA_HEADS
D_FF = 4 * D_MODEL

ML_QK = MLSTM_HEADS * MLSTM_DQK
ML_V = MLSTM_HEADS * MLSTM_DV
SB_W = SB_HEADS * SB_DH
AB_SIZES = (ML_QK, ML_QK, ML_V, ML_V, MLSTM_HEADS, MLSTM_HEADS, SB_W, SB_W, SB_W)
AB_IN = sum(AB_SIZES)
AB_OUT = ML_V + SB_W

GDN_QK = GDN_HEADS * GDN_DK
GDN_VW = GDN_HEADS * GDN_DV
GDN_CONV_CH = 2 * GDN_QK + GDN_VW
C_SIZES = (GDN_CONV_CH, GDN_VW, GDN_HEADS, GDN_HEADS)
C_IN = sum(C_SIZES)

kernel_name = 'hybrid_mlstm_stickbreak_gdn_memxattn'


def _split(y, sizes):
    return jnp.split(y, np.cumsum(sizes)[:-1].tolist(), axis=-1)


def rmsnorm(x, g):
    xf = x.astype(jnp.float32)
    y = xf * lax.rsqrt(jnp.mean(xf * xf, axis=-1, keepdims=True) + EPS)
    return (y * g.astype(jnp.float32)).astype(x.dtype)


def l2norm(x):
    xf = x.astype(jnp.float32)
    return xf * lax.rsqrt(jnp.sum(xf * xf, axis=-1, keepdims=True) + EPS)


def _to_chunks(a, L):
    B, T = a.shape[0], a.shape[1]
    a = a.reshape((B, T // L, L) + a.shape[2:])
    return jnp.moveaxis(a, (1, 3), (0, 2))


def _from_chunks(a):
    a = jnp.moveaxis(a, (0, 2), (1, 3))
    nc, L = a.shape[1], a.shape[2]
    return a.reshape((a.shape[0], nc * L) + a.shape[3:])


def mlstm(q, k, v, i_pre, f_pre):
    f32 = jnp.float32
    B, T, H, dk = q.shape
    dv = v.shape[-1]
    L = MLSTM_CHUNK
    log_i = GATE_SOFTCAP * jnp.tanh(i_pre.astype(f32) / GATE_SOFTCAP)
    log_f = jax.nn.log_sigmoid(f_pre.astype(f32))
    xs = (_to_chunks(q.astype(f32), L),
          _to_chunks(k.astype(f32) * (dk ** -0.5), L),
          _to_chunks(v.astype(f32), L),
          _to_chunks(log_i, L),
          _to_chunks(log_f, L))
    causal = jnp.tril(jnp.ones((L, L), dtype=bool))

    def step(carry, inp):
        C, n, m = carry
        qc, kc, vc, li, lf = inp
        b = jnp.cumsum(lf, axis=-1)
        d = jnp.where(causal, b[..., :, None] - b[..., None, :] + li[..., None, :], -jnp.inf)
        inter = b + m[..., None]
        m_t = jnp.maximum(inter, jnp.max(d, axis=-1))
        w_inter = jnp.exp(inter - m_t)
        s = jnp.einsum('bhtd,bhsd->bhts', qc, kc) * jnp.exp(d - m_t[..., None])
        num = jnp.einsum('bhts,bhsv->bhtv', s, vc) + w_inter[..., None] * jnp.einsum('bhvd,bhtd->bhtv', C, qc)
        den = jnp.sum(s, axis=-1) + w_inter * jnp.einsum('bhd,bhtd->bht', n, qc)
        h = num / jnp.maximum(jnp.abs(den), jnp.exp(-m_t))[..., None]
        m_new = m_t[..., -1]
        w_old = jnp.exp(b[..., -1] + m - m_new)
        w_s = jnp.exp(b[..., -1:] - b + li - m_new[..., None])
        C = w_old[..., None, None] * C + jnp.einsum('bhsv,bhsd->bhvd', vc * w_s[..., None], kc)
        n = w_old[..., None] * n + jnp.einsum('bhs,bhsd->bhd', w_s, kc)
        return (C, n, m_new), h

    init = (jnp.zeros((B, H, dv, dk), f32), jnp.zeros((B, H, dk), f32), jnp.zeros((B, H), f32))
    _, h = lax.scan(step, init, xs)
    return _from_chunks(h)


def stick_breaking(q, k, v):
    f32 = jnp.float32
    B, T, H, dh = q.shape
    scale = dh ** -0.5
    qf, kf, vf = q.astype(f32), k.astype(f32), v.astype(f32)
    outs = []
    for t0 in range(0, T, SB_BLOCK):
        t1 = t0 + SB_BLOCK
        z = jnp.einsum('bqhd,bkhd->bhqk', qf[:, t0:t1], kf[:, :t1]) * scale
        qpos = t0 + jnp.arange(SB_BLOCK)
        kpos = jnp.arange(t1)
        strict = kpos[None, :] < qpos[:, None]
        log_beta = jnp.where(strict, jax.nn.log_sigmoid(z), -jnp.inf)
        log_1m = jnp.where(strict, jax.nn.log_sigmoid(-z), 0.0)
        acc = lax.cumsum(log_1m, axis=3, reverse=True) - log_1m
        w = jnp.exp(log_beta + acc)
        outs.append(jnp.einsum('bhqk,bkhd->bqhd', w, vf[:, :t1]))
    return jnp.concatenate(outs, axis=1)


def short_conv(x, w):
    K = w.shape[0]
    T = x.shape[1]
    xp = jnp.pad(x, ((0, 0), (K - 1, 0), (0, 0)))
    return sum(xp[:, j:j + T] * w[j] for j in range(K))


def gated_deltanet(q, k, v, beta, log_alpha):
    f32 = jnp.float32
    B, T, H, dk = q.shape
    dv = v.shape[-1]
    L = GDN_CHUNK
    qc = _to_chunks(q.astype(f32) * (dk ** -0.5), L)
    kc = _to_chunks(k.astype(f32), L)
    vc = _to_chunks(v.astype(f32), L)
    bc = _to_chunks(beta.astype(f32), L)
    g = jnp.cumsum(_to_chunks(log_alpha.astype(f32), L), axis=-1)
    incl = jnp.tril(jnp.ones((L, L), dtype=bool))
    strict = jnp.tril(jnp.ones((L, L), dtype=bool), k=-1)
    decay = jnp.where(incl, jnp.exp(jnp.where(incl, g[..., :, None] - g[..., None, :], 0.0)), 0.0)
    kk = jnp.einsum('nbhtd,nbhsd->nbhts', kc, kc)
    lower = jnp.where(strict, bc[..., :, None] * kk * decay, 0.0)
    eye = jnp.eye(L, dtype=f32)
    rhs = jnp.concatenate([vc * bc[..., None], kc * (bc * jnp.exp(g))[..., None]], axis=-1)
    sol = lax.linalg.triangular_solve(lower + eye, rhs, left_side=True, lower=True, unit_diagonal=True)
    u_c, w_c = sol[..., :dv], sol[..., dv:]
    attn = jnp.einsum('nbhtd,nbhsd->nbhts', qc, kc) * decay
    q_dec = qc * jnp.exp(g)[..., None]
    g_last = g[..., -1]
    k_dec = kc * jnp.exp(g_last[..., None] - g)[..., None]

    def step(S, inp):
        u, wk, att, qd, kd, gl = inp
        v_new = u - jnp.einsum('bhtk,bhkv->bhtv', wk, S)
        o = jnp.einsum('bhtk,bhkv->bhtv', qd, S) + jnp.einsum('bhts,bhsv->bhtv', att, v_new)
        S = S * jnp.exp(gl)[..., None, None] + jnp.einsum('bhsk,bhsv->bhkv', kd, v_new)
        return S, o

    S0 = jnp.zeros((B, H, dk, dv), f32)
    _, o = lax.scan(step, S0, (u_c, w_c, attn, q_dec, k_dec, g_last))
    return _from_chunks(o)


def ab_mixer(h, w_in, b_i, b_f, head_gain, w_out):
    B, T, _ = h.shape
    mq, mk, mv, mo, mi, mf, sq, sk, sv = _split(h @ w_in, AB_SIZES)
    hm = mlstm(mq.reshape(B, T, MLSTM_HEADS, MLSTM_DQK),
               mk.reshape(B, T, MLSTM_HEADS, MLSTM_DQK),
               mv.reshape(B, T, MLSTM_HEADS, MLSTM_DV),
               mi + b_i, mf + b_f)
    hm = rmsnorm(hm, head_gain) * jax.nn.sigmoid(mo.reshape(B, T, MLSTM_HEADS, MLSTM_DV).astype(jnp.float32))
    hs = stick_breaking(sq.reshape(B, T, SB_HEADS, SB_DH),
                        sk.reshape(B, T, SB_HEADS, SB_DH),
                        sv.reshape(B, T, SB_HEADS, SB_DH))
    out = jnp.concatenate([hm.reshape(B, T, ML_V), hs.reshape(B, T, SB_W)], axis=-1).astype(h.dtype)
    return out @ w_out


def c_mixer(h, w_in, conv_w, a_log, dt_bias, head_gain, w_out):
    B, T, _ = h.shape
    qkv, gate, b, a = _split(h @ w_in, C_SIZES)
    qkv = jax.nn.silu(short_conv(qkv, conv_w))
    q, k, v = _split(qkv, (GDN_QK, GDN_QK, GDN_VW))
    q = l2norm(q.reshape(B, T, GDN_HEADS, GDN_DK))
    k = l2norm(k.reshape(B, T, GDN_HEADS, GDN_DK))
    v = v.reshape(B, T, GDN_HEADS, GDN_DV)
    beta = jax.nn.sigmoid(b.astype(jnp.float32))
    log_alpha = -jnp.exp(a_log.astype(jnp.float32)) * jax.nn.softplus(a.astype(jnp.float32) + dt_bias.astype(jnp.float32))
    o = gated_deltanet(q, k, v, beta, log_alpha)
    o = rmsnorm(o, head_gain) * jax.nn.silu(gate.reshape(B, T, GDN_HEADS, GDN_DV).astype(jnp.float32))
    return o.reshape(B, T, GDN_VW).astype(h.dtype) @ w_out


def mem_cross_attn(h, memn, wq, wk, wv, wo):
    B, T, _ = h.shape
    M = memn.shape[1]
    q = (h @ wq).reshape(B, T, XA_HEADS, XA_DH)
    k = (memn @ wk).reshape(B, M, XA_HEADS, XA_DH)
    v = (memn @ wv).reshape(B, M, XA_HEADS, XA_DH)
    s = jnp.einsum('bthd,bmhd->bhtm', q, k).astype(jnp.float32) * (XA_DH ** -0.5)
    p = jax.nn.softmax(s, axis=-1).astype(v.dtype)
    o = jnp.einsum('bhtm,bmhd->bthd', p, v).reshape(B, T, XA_HEADS * XA_DH)
    return o @ wo


def sq_relu_mlp(h, w1, w2):
    return jnp.square(jax.nn.relu(h @ w1)) @ w2


def setup_inputs(seed: int = 0) -> dict:
    key = jax.random.key(seed)
    ks = jax.random.split(key, 32)
    f32 = jnp.float32

    def w(k, shape, fan_in):
        return jax.random.normal(k, shape, f32) * (fan_in ** -0.5)

    def gain(k, shape):
        return 1.0 + 0.02 * jax.random.normal(k, shape, f32)

    x = jax.random.normal(ks[0], (BATCH, SEQ, D_MODEL), f32)
    mem = jax.random.normal(ks[1], (BATCH, MEM_LEN, D_MODEL), f32)
    mix_norm = gain(ks[2], (DEPTH, D_MODEL))
    ab_w_in = w(ks[3], (N_EVEN, D_MODEL, AB_IN), D_MODEL)
    ab_b_i = 0.1 * jax.random.normal(ks[4], (N_EVEN, MLSTM_HEADS), f32)
    ab_b_f = jnp.linspace(3.0, 6.0, MLSTM_HEADS, dtype=f32)[None, :] + 0.1 * jax.random.normal(ks[5], (N_EVEN, MLSTM_HEADS), f32)
    ab_head_gain = gain(ks[6], (N_EVEN, MLSTM_HEADS, MLSTM_DV))
    ab_w_out = w(ks[7], (N_EVEN, AB_OUT, D_MODEL), AB_OUT)
    c_w_in = w(ks[8], (N_ODD, D_MODEL, C_IN), D_MODEL)
    c_conv_w = w(ks[9], (N_ODD, GDN_CONV, GDN_CONV_CH), GDN_CONV)
    c_a_log = jnp.log(jax.random.uniform(ks[10], (N_ODD, GDN_HEADS), f32, 1.0, 16.0))
    dt = jnp.exp(jax.random.uniform(ks[11], (N_ODD, GDN_HEADS), f32, float(np.log(1e-3)), float(np.log(1e-1))))
    c_dt_bias = dt + jnp.log(-jnp.expm1(-dt))
    c_head_gain = gain(ks[12], (N_ODD, GDN_DV))
    c_w_out = w(ks[13], (N_ODD, GDN_VW, D_MODEL), GDN_VW)
    xa_norm = gain(ks[14], (DEPTH, D_MODEL))
    mem_norm = gain(ks[15], (D_MODEL,))
    xa_wq = w(ks[16], (DEPTH, D_MODEL, D_MODEL), D_MODEL)
    xa_wk = w(ks[17], (DEPTH, D_MODEL, D_MODEL), D_MODEL)
    xa_wv = w(ks[18], (DEPTH, D_MODEL, D_MODEL), D_MODEL)
    xa_wo = w(ks[19], (DEPTH, D_MODEL, D_MODEL), D_MODEL)
    mlp_norm = gain(ks[20], (DEPTH, D_MODEL))
    mlp_w1 = w(ks[21], (DEPTH, D_MODEL, D_FF), D_MODEL)
    mlp_w2 = w(ks[22], (DEPTH, D_FF, D_MODEL), D_FF)
    final_norm = gain(ks[23], (D_MODEL,))
    return {'x': x, 'mem': mem, 'mix_norm': mix_norm,
            'ab_w_in': ab_w_in, 'ab_b_i': ab_b_i, 'ab_b_f': ab_b_f,
            'ab_head_gain': ab_head_gain, 'ab_w_out': ab_w_out,
            'c_w_in': c_w_in, 'c_conv_w': c_conv_w, 'c_a_log': c_a_log,
            'c_dt_bias': c_dt_bias, 'c_head_gain': c_head_gain, 'c_w_out': c_w_out,
            'xa_norm': xa_norm, 'mem_norm': mem_norm, 'xa_wq': xa_wq, 'xa_wk': xa_wk,
            'xa_wv': xa_wv, 'xa_wo': xa_wo, 'mlp_norm': mlp_norm, 'mlp_w1': mlp_w1,
            'mlp_w2': mlp_w2, 'final_norm': final_norm}


def reference(x, mem, mix_norm, ab_w_in, ab_b_i, ab_b_f, ab_head_gain, ab_w_out,
              c_w_in, c_conv_w, c_a_log, c_dt_bias, c_head_gain, c_w_out,
              xa_norm, mem_norm, xa_wq, xa_wk, xa_wv, xa_wo,
              mlp_norm, mlp_w1, mlp_w2, final_norm):
    memn = rmsnorm(mem, mem_norm)
    for l in range(DEPTH):
        j = l // 2
        h = rmsnorm(x, mix_norm[l])
        if l % 2 == 0:
            x = x + ab_mixer(h, ab_w_in[j], ab_b_i[j], ab_b_f[j], ab_head_gain[j], ab_w_out[j])
        else:
            x = x + c_mixer(h, c_w_in[j], c_conv_w[j], c_a_log[j], c_dt_bias[j], c_head_gain[j], c_w_out[j])
        x = x + mem_cross_attn(rmsnorm(x, xa_norm[l]), memn, xa_wq[l], xa_wk[l], xa_wv[l], xa_wo[l])
        x = x + sq_relu_mlp(rmsnorm(x, mlp_norm[l]), mlp_w1[l], mlp_w2[l])
    return rmsnorm(x, final_norm)
```

```python
import functools

import jax
import jax.numpy as jnp
from jax import lax
from jax.experimental import pallas as pl
from jax.experimental.pallas import tpu as pltpu

F32 = jnp.float32
BF16 = jnp.bfloat16

EPS = 1e-6
GATE_SOFTCAP = 15.0

MLSTM_HEADS = 4
MLSTM_DQK = 64
MLSTM_DV = 128
SB_HEADS = 4
SB_DH = 128
GDN_HEADS = 8
GDN_DK = 128
GDN_DV = 128
GDN_CONV = 4
GDN_CHUNK = 64
XA_HEADS = 4

LANES = 128
VMEM_LIMIT_BYTES = 48 * 1024 * 1024

NT_DIMS = (((1,), (1,)), ((), ()))
TN_DIMS = (((0,), (0,)), ((), ()))


def _params(*semantics):
    return pltpu.CompilerParams(dimension_semantics=semantics, vmem_limit_bytes=VMEM_LIMIT_BYTES)


def _dot(a, b):
    return jnp.dot(a, b, preferred_element_type=F32)


def _dot_nt(a, b):
    return lax.dot_general(a, b, NT_DIMS, preferred_element_type=F32)


def _dot_tn(a, b):
    return lax.dot_general(a, b, TN_DIMS, preferred_element_type=F32)


def _split_bf16(a):
    hi = a.astype(BF16)
    lo = (a - hi.astype(F32)).astype(BF16)
    return hi, lo


def _dot3(a, b):
    a_hi, a_lo = _split_bf16(a)
    b_hi, b_lo = _split_bf16(b)
    return _dot(a_hi, b_hi) + (_dot(a_lo, b_hi) + _dot(a_hi, b_lo))


def _rms(x, g):
    return x * lax.rsqrt(jnp.mean(x * x, axis=-1, keepdims=True) + EPS) * g


def _sigmoid(x):
    return 1.0 / (1.0 + jnp.exp(-x))


def _softplus(x):
    return jnp.maximum(x, 0.0) + jnp.log(1.0 + jnp.exp(-jnp.abs(x)))


def _chunk_cumsum(x, chunk):
    pos = lax.broadcasted_iota(jnp.int32, x.shape, 1) % chunk
    shift = 1
    while shift < chunk:
        x = x + jnp.where(pos >= shift, pltpu.roll(x, shift, axis=1), 0.0)
        shift *= 2
    return x


def _col_from_row(row, eye):
    n = row.shape[1]
    return jnp.sum(jnp.where(eye, jnp.broadcast_to(row, (n, n)), 0.0), axis=1, keepdims=True)


PROJ_COL_CHUNK = 1024


def _norm_proj_body(*refs, n_out, has_gate):
    x_ref, g_ref = refs[0], refs[1]
    w_refs = refs[2:2 + n_out]
    pos = 2 + n_out
    wg_ref = refs[pos] if has_gate else None
    pos += int(has_gate)
    out_refs = refs[pos:pos + n_out]
    pos += n_out
    h = _rms(x_ref[...], g_ref[...]).astype(BF16)
    for w_ref, o_ref in zip(w_refs, out_refs):
        width = w_ref.shape[1]
        for c0 in range(0, width, PROJ_COL_CHUNK):
            c1 = min(c0 + PROJ_COL_CHUNK, width)
            o_ref[:, c0:c1] = _dot(h, w_ref[:, c0:c1]).astype(o_ref.dtype)
    if has_gate:
        refs[pos][0] = _dot_nt(wg_ref[...], h)


def norm_proj(x, gain, weights, out_dtypes, gate_w_t=None, seq_len=None, tm=512):
    n, d = x.shape
    has_gate = gate_w_t is not None
    in_specs = [pl.BlockSpec((tm, d), lambda i: (i, 0)), pl.BlockSpec((1, d), lambda i: (0, 0))]
    in_specs += [pl.BlockSpec(w.shape, lambda i: (0, 0)) for w in weights]
    args = [x, gain.reshape(1, d)] + list(weights)
    out_shape = [jax.ShapeDtypeStruct((n, w.shape[1]), dt) for w, dt in zip(weights, out_dtypes)]
    out_specs = [pl.BlockSpec((tm, w.shape[1]), lambda i: (i, 0)) for w in weights]
    if has_gate:
        n_gates = gate_w_t.shape[0]
        tiles_per_seq = seq_len // tm
        in_specs.append(pl.BlockSpec(gate_w_t.shape, lambda i: (0, 0)))
        args.append(gate_w_t)
        out_shape.append(jax.ShapeDtypeStruct((n // seq_len, n_gates, seq_len), F32))
        out_specs.append(pl.BlockSpec((1, n_gates, tm), lambda i: (i // tiles_per_seq, 0, i % tiles_per_seq)))
    return pl.pallas_call(
        functools.partial(_norm_proj_body, n_out=len(weights), has_gate=has_gate),
        grid=(n // tm,),
        in_specs=in_specs,
        out_specs=out_specs,
        out_shape=out_shape,
        compiler_params=_params("parallel"),
        name="norm_proj",
    )(*args)


def _proj_res_body(*refs, n_in):
    x_ref = refs[0]
    a_refs = refs[1:1 + n_in]
    w_refs = refs[1 + n_in:1 + 2 * n_in]
    o_ref = refs[1 + 2 * n_in]
    acc = x_ref[...]
    for a_ref, w_ref in zip(a_refs, w_refs):
        acc = acc + _dot(a_ref[...], w_ref[...])
    o_ref[...] = acc


def proj_res(x, acts, weights, tm=512):
    n, d = x.shape
    in_specs = [pl.BlockSpec((tm, d), lambda i: (i, 0))]
    in_specs += [pl.BlockSpec((tm, a.shape[1]), lambda i: (i, 0)) for a in acts]
    in_specs += [pl.BlockSpec(w.shape, lambda i: (0, 0)) for w in weights]
    return pl.pallas_call(
        functools.partial(_proj_res_body, n_in=len(acts)),
        grid=(n // tm,),
        in_specs=in_specs,
        out_specs=pl.BlockSpec((tm, d), lambda i: (i, 0)),
        out_shape=jax.ShapeDtypeStruct((n, d), F32),
        compiler_params=_params("parallel"),
        name="proj_res",
    )(x, *acts, *weights)


MLSTM_BLOCK = 512
MLSTM_CC = 128


def _mlstm_body(q_ref, k_ref, v_ref, og_ref, gate_ref, bias_ref, gain_ref, o_ref, c_scr, m_scr, ab_scr):
    n_heads = MLSTM_HEADS
    cc = MLSTM_CC
    block = q_ref.shape[1]

    @pl.when(pl.program_id(1) == 0)
    def _():
        c_scr[...] = jnp.zeros_like(c_scr)
        m_scr[...] = jnp.zeros_like(m_scr)

    pre = gate_ref[0] + bias_ref[...]
    log_i = GATE_SOFTCAP * jnp.tanh(pre[:n_heads] / GATE_SOFTCAP)
    log_f = -_softplus(-pre[n_heads:])
    b = _chunk_cumsum(log_f, cc)
    ab_scr[:n_heads] = log_i - b
    ab_scr[n_heads:] = b

    r_idx = lax.broadcasted_iota(jnp.int32, (cc, cc), 0)
    c_idx = lax.broadcasted_iota(jnp.int32, (cc, cc), 1)
    eye = r_idx == c_idx
    causal = c_idx <= r_idx
    ones_col = (lax.broadcasted_iota(jnp.int32, (cc, LANES), 1) == 0).astype(BF16)

    def chunk(ci, carry):
        r0 = pl.multiple_of(ci * cc, cc)
        rows = pl.ds(r0, cc)
        for h in range(n_heads):
            lanes = slice(h * LANES, (h + 1) * LANES)
            q = q_ref[0, rows, lanes]
            k = k_ref[0, rows, lanes]
            v = v_ref[0, rows, lanes]
            a_row = ab_scr[h:h + 1, rows]
            b_row = ab_scr[n_heads + h:n_heads + h + 1, rows]
            a_col = _col_from_row(a_row, eye)
            b_col = _col_from_row(b_row, eye)
            m_prev = m_scr[h:h + 1, 0:1]
            d = jnp.where(causal, jnp.broadcast_to(a_row, (cc, cc)), -jnp.inf)
            g_col = jnp.maximum(jnp.max(d, axis=1, keepdims=True), m_prev)
            p = jnp.exp(d - g_col)
            w_inter = jnp.exp(m_prev - g_col)
            s = (_dot_nt(q, k) * p).astype(BF16)
            v_aug = jnp.concatenate([v, ones_col], axis=1)
            c_aug = c_scr[h]
            numden = _dot(s, v_aug) + w_inter * _dot(q, c_aug.astype(BF16))
            num = numden[:, :MLSTM_DV]
            den = numden[:, MLSTM_DV:MLSTM_DV + 1]
            hh = num / jnp.maximum(jnp.abs(den), jnp.exp(-(b_col + g_col)))
            g_last = g_col[cc - 1:cc]
            w_old = jnp.exp(m_prev - g_last)
            w_s = jnp.exp(a_col - g_last)
            c_scr[h] = w_old * c_aug + _dot_tn(k, (v_aug.astype(F32) * w_s).astype(BF16))
            m_scr[h:h + 1, :] = jnp.broadcast_to(b_col[cc - 1:cc] + g_last, (1, LANES))
            hn = _rms(hh, gain_ref[:, lanes]) * _sigmoid(og_ref[0, rows, lanes])
            o_ref[0, rows, lanes] = hn.astype(o_ref.dtype)
        return carry

    lax.fori_loop(0, block // cc, chunk, 0)


def mlstm(proj, og, gates, bias, gain, q_blk, k_blk, v_blk):
    bsz, seq, _ = proj.shape
    width = MLSTM_HEADS * LANES
    blk = MLSTM_BLOCK

    def col(cb):
        return pl.BlockSpec((1, blk, width), lambda b, t: (b, t, cb))

    return pl.pallas_call(
        _mlstm_body,
        grid=(bsz, seq // blk),
        in_specs=[col(q_blk), col(k_blk), col(v_blk),
                  pl.BlockSpec((1, blk, width), lambda b, t: (b, t, 0)),
                  pl.BlockSpec((1, 2 * MLSTM_HEADS, blk), lambda b, t: (b, 0, t)),
                  pl.BlockSpec((2 * MLSTM_HEADS, 1), lambda b, t: (0, 0)),
                  pl.BlockSpec((1, width), lambda b, t: (0, 0))],
        out_specs=pl.BlockSpec((1, blk, width), lambda b, t: (b, t, 0)),
        out_shape=jax.ShapeDtypeStruct((bsz, seq, width), BF16),
        scratch_shapes=[pltpu.VMEM((MLSTM_HEADS, LANES, 2 * LANES), F32),
                        pltpu.VMEM((8, LANES), F32),
                        pltpu.VMEM((2 * MLSTM_HEADS, blk), F32)],
        compiler_params=_params("parallel", "arbitrary"),
        name="mlstm",
    )(proj, proj, proj, og, gates, bias, gain)


SB_TQ = 256
SB_TK = 128


def _sb_body(q_ref, k_ref, v_ref, o_ref, acc_scr, run_scr, *, scale):
    tq = q_ref.shape[1]
    tk = SB_TK
    sub = tq // tk
    i = pl.program_id(2)
    q = q_ref[0]
    acc_scr[...] = jnp.zeros_like(acc_scr)
    run_scr[...] = jnp.zeros_like(run_scr)

    r_idx = lax.broadcasted_iota(jnp.int32, (tk, 2 * tk), 0)
    c_idx = lax.broadcasted_iota(jnp.int32, (tk, 2 * tk), 1)
    suffix = ((r_idx > c_idx) | (c_idx >= tk)).astype(BF16)

    def tile(k_t, v_t, mask):
        z = _dot_nt(q, k_t) * scale
        sp = jnp.log(1.0 + jnp.exp(-jnp.abs(z)))
        log_1m = -jnp.maximum(z, 0.0) - sp
        log_beta = log_1m + z
        if mask is not None:
            log_1m = jnp.where(mask, log_1m, 0.0)
        hi, lo = _split_bf16(log_1m)
        sums = _dot(hi, suffix) + _dot(lo, suffix)
        run = run_scr[...]
        w = jnp.exp(log_beta + sums[:, :tk] + run)
        if mask is not None:
            w = jnp.where(mask, w, 0.0)
        acc_scr[...] += _dot(w.astype(BF16), v_t)
        run_scr[...] = run + sums[:, tk:]

    q_pos = lax.broadcasted_iota(jnp.int32, (tq, tk), 0)
    k_pos = lax.broadcasted_iota(jnp.int32, (tq, tk), 1)
    for jj in reversed(range(sub)):
        rows = pl.ds(pl.multiple_of(i * tq + jj * tk, tk), tk)
        tile(k_ref[0, rows, :], v_ref[0, rows, :], k_pos + jj * tk < q_pos)

    n_tiles = i * sub

    def body(step, carry):
        rows = pl.ds(pl.multiple_of((n_tiles - 1 - step) * tk, tk), tk)
        tile(k_ref[0, rows, :], v_ref[0, rows, :], None)
        return carry

    lax.fori_loop(0, n_tiles, body, 0)
    o_ref[0] = acc_scr[...].astype(o_ref.dtype)


def stick_breaking(proj, q_blk, k_blk, v_blk):
    bsz, seq, _ = proj.shape
    tq = SB_TQ
    return pl.pallas_call(
        functools.partial(_sb_body, scale=SB_DH ** -0.5),
        grid=(bsz, SB_HEADS, seq // tq),
        in_specs=[pl.BlockSpec((1, tq, SB_DH), lambda b, h, i: (b, i, q_blk + h)),
                  pl.BlockSpec((1, seq, SB_DH), lambda b, h, i: (b, 0, k_blk + h)),
                  pl.BlockSpec((1, seq, SB_DH), lambda b, h, i: (b, 0, v_blk + h))],
        out_specs=pl.BlockSpec((1, tq, SB_DH), lambda b, h, i: (b, i, h)),
        out_shape=jax.ShapeDtypeStruct((bsz, seq, SB_HEADS * SB_DH), BF16),
        scratch_shapes=[pltpu.VMEM((tq, SB_DH), F32), pltpu.VMEM((tq, SB_TK), F32)],
        compiler_params=_params("parallel", "parallel", "arbitrary"),
        name="stickbreak",
    )(proj, proj, proj)


GDN_BLOCK = 256
GDN_PAIR = 2 * GDN_CHUNK
CONV_PAD = 8
NEUMANN_STEPS = 5


def _gdn_body(q_ref, k_ref, v_ref, og_ref, gate_ref, cwq_ref, cwk_ref, cwv_ref, alog_ref, dtb_ref, gain_ref,
              o_ref, s_scr, ext_scr, u_scr, w_scr, att_scr, qd_scr, kd_scr, gc_scr, oc_scr):
    block = q_ref.shape[1]
    chunk = GDN_CHUNK
    pair = GDN_PAIR
    head = pl.program_id(1)

    @pl.when(pl.program_id(2) == 0)
    def _():
        s_scr[...] = jnp.zeros_like(s_scr)
        ext_scr[:, 0:CONV_PAD, :] = jnp.zeros((3, CONV_PAD, LANES), F32)

    n_heads = GDN_HEADS
    pre = gate_ref[0]
    beta_all = _sigmoid(pre[:n_heads])
    log_alpha = -jnp.exp(alog_ref[...]) * _softplus(pre[n_heads:] + dtb_ref[...])
    is_head = lax.broadcasted_iota(jnp.int32, (n_heads, block), 0) == head
    g_head = jnp.sum(jnp.where(is_head, _chunk_cumsum(log_alpha, chunk), 0.0), axis=0, keepdims=True)
    beta_head = jnp.sum(jnp.where(is_head, beta_all, 0.0), axis=0, keepdims=True)

    def conv_silu(idx, x_ref, cw_ref):
        x = x_ref[0]
        ext_scr[idx, CONV_PAD:CONV_PAD + block, :] = x
        y = jnp.zeros((block, LANES), F32)
        for j in range(GDN_CONV):
            off = CONV_PAD - (GDN_CONV - 1) + j
            y = y + ext_scr[idx, off:off + block, :] * cw_ref[j:j + 1, :]
        ext_scr[idx, 0:CONV_PAD, :] = x[block - CONV_PAD:, :]
        return y * _sigmoid(y)

    def l2n(x):
        return x * lax.rsqrt(jnp.sum(x * x, axis=-1, keepdims=True) + EPS)

    q_all = l2n(conv_silu(0, q_ref, cwq_ref)) * (GDN_DK ** -0.5)
    k_all = l2n(conv_silu(1, k_ref, cwk_ref))
    v_all = conv_silu(2, v_ref, cwv_ref)

    r_idx = lax.broadcasted_iota(jnp.int32, (pair, pair), 0)
    c_idx = lax.broadcasted_iota(jnp.int32, (pair, pair), 1)
    eye = r_idx == c_idx
    same = (r_idx // chunk) == (c_idx // chunk)
    incl = same & (c_idx <= r_idx)
    strict = same & (c_idx < r_idx)
    eye_f = eye.astype(F32)
    first = lax.broadcasted_iota(jnp.int32, (pair, 1), 0) < chunk

    for p in range(block // pair):
        rows = slice(p * pair, (p + 1) * pair)
        q = q_all[rows]
        k = k_all[rows]
        v = v_all[rows]
        g_row = g_head[:, rows]
        beta_row = beta_head[:, rows]
        g_col = _col_from_row(g_row, eye)
        beta_col = _col_from_row(beta_row, eye)
        decay = jnp.where(incl, jnp.exp(jnp.where(incl, g_col - g_row, 0.0)), 0.0)
        k_b = k.astype(BF16)
        kk = _dot_nt(k_b, k_b)
        qk = _dot_nt(q.astype(BF16), k_b)
        low = jnp.where(strict, beta_col * kk * decay, 0.0)
        x = -low
        t_inv = eye_f + x
        pw = x
        for _ in range(NEUMANN_STEPS):
            pw = _dot3(pw, pw)
            t_inv = t_inv + _dot3(t_inv, pw)
        exp_g = jnp.exp(g_col)
        rhs = jnp.concatenate([v * beta_col, k * (beta_col * exp_g)], axis=1)
        sol = _dot3(t_inv, rhs)
        g_last = jnp.where(first, g_col[chunk - 1:chunk], g_col[pair - 1:pair])
        u_scr[rows] = sol[:, :GDN_DV]
        w_scr[rows] = sol[:, GDN_DV:].astype(BF16)
        att_scr[rows] = (qk * decay).astype(BF16)
        qd_scr[rows] = (q * exp_g).astype(BF16)
        kd_scr[rows] = (k * jnp.exp(g_last - g_col)).astype(BF16)
        gc_scr[rows] = jnp.broadcast_to(g_col, (pair, LANES))

    for c in range(block // chunk):
        rows = slice(c * chunk, (c + 1) * chunk)
        lanes = slice((c % 2) * chunk, (c % 2) * chunk + chunk)
        s = s_scr[...]
        s_b = s.astype(BF16)
        v_new = u_scr[rows] - _dot(w_scr[rows], s_b)
        v_b = v_new.astype(BF16)
        oc_scr[rows] = _dot(qd_scr[rows], s_b) + _dot(att_scr[rows, lanes], v_b)
        decay_last = jnp.exp(gc_scr[(c + 1) * chunk - 1:(c + 1) * chunk, :])
        s_scr[...] = s * decay_last + _dot_tn(kd_scr[rows], v_b)

    og = og_ref[0]
    o_ref[0] = (_rms(oc_scr[...], gain_ref[...]) * (og * _sigmoid(og))).astype(o_ref.dtype)


def gated_deltanet(proj, gates, conv_w, a_log, dt_bias, gain):
    bsz, seq, _ = proj.shape
    n_heads = GDN_HEADS
    blk = GDN_BLOCK

    def col(group):
        return pl.BlockSpec((1, blk, LANES), lambda b, h, t: (b, t, group * n_heads + h))

    def conv_spec(group):
        return pl.BlockSpec((GDN_CONV, LANES), lambda b, h, t: (0, group * n_heads + h))

    small = pl.BlockSpec((n_heads, 1), lambda b, h, t: (0, 0))
    return pl.pallas_call(
        _gdn_body,
        grid=(bsz, n_heads, seq // blk),
        in_specs=[col(0), col(1), col(2), col(3),
                  pl.BlockSpec((1, 2 * n_heads, blk), lambda b, h, t: (b, 0, t)),
                  conv_spec(0), conv_spec(1), conv_spec(2), small, small,
                  pl.BlockSpec((1, GDN_DV), lambda b, h, t: (0, 0))],
        out_specs=pl.BlockSpec((1, blk, GDN_DV), lambda b, h, t: (b, t, h)),
        out_shape=jax.ShapeDtypeStruct((bsz, seq, n_heads * GDN_DV), BF16),
        scratch_shapes=[pltpu.VMEM((GDN_DK, GDN_DV), F32),
                        pltpu.VMEM((3, CONV_PAD + blk, LANES), F32),
                        pltpu.VMEM((blk, GDN_DV), F32),
                        pltpu.VMEM((blk, GDN_DK), BF16),
                        pltpu.VMEM((blk, GDN_PAIR), BF16),
                        pltpu.VMEM((blk, GDN_DK), BF16),
                        pltpu.VMEM((blk, GDN_DK), BF16),
                        pltpu.VMEM((blk, LANES), F32),
                        pltpu.VMEM((blk, GDN_DV), F32)],
        compiler_params=_params("parallel", "parallel", "arbitrary"),
        name="gdn",
    )(proj, proj, proj, proj, gates, conv_w, conv_w, conv_w, a_log, dt_bias, gain)


def _xattn_body(x_ref, g_ref, wq_ref, k_ref, v_ref, wo_ref, o_ref, *, n_heads):
    x = x_ref[...]
    d = x.shape[1]
    dh = d // n_heads
    h = _rms(x, g_ref[...]).astype(BF16)
    q = _dot(h, wq_ref[...]).astype(BF16)
    outs = []
    for hd in range(n_heads):
        lanes = slice(hd * dh, (hd + 1) * dh)
        s = _dot_nt(q[:, lanes], k_ref[0, :, lanes]) * (dh ** -0.5)
        e = jnp.exp(s - jnp.max(s, axis=-1, keepdims=True))
        p = e / jnp.sum(e, axis=-1, keepdims=True)
        outs.append(_dot(p.astype(BF16), v_ref[0, :, lanes]).astype(BF16))
    o_ref[...] = x + _dot(jnp.concatenate(outs, axis=1), wo_ref[...])


def xattn(x, gain, wq, k, v, wo, seq_len, tm=512):
    n, d = x.shape
    mem_len = k.shape[1]
    tiles_per_seq = seq_len // tm
    return pl.pallas_call(
        functools.partial(_xattn_body, n_heads=XA_HEADS),
        grid=(n // tm,),
        in_specs=[pl.BlockSpec((tm, d), lambda i: (i, 0)),
                  pl.BlockSpec((1, d), lambda i: (0, 0)),
                  pl.BlockSpec((d, d), lambda i: (0, 0)),
                  pl.BlockSpec((1, mem_len, d), lambda i: (i // tiles_per_seq, 0, 0)),
                  pl.BlockSpec((1, mem_len, d), lambda i: (i // tiles_per_seq, 0, 0)),
                  pl.BlockSpec((d, d), lambda i: (0, 0))],
        out_specs=pl.BlockSpec((tm, d), lambda i: (i, 0)),
        out_shape=jax.ShapeDtypeStruct((n, d), F32),
        compiler_params=_params("parallel"),
        name="xattn",
    )(x, gain.reshape(1, d), wq, k, v, wo)


def _mlp_body(x_ref, g_ref, w1_ref, w2_ref, fg_ref, o_ref, h_scr, acc_scr, *, final_norm):
    f = pl.program_id(1)

    @pl.when(f == 0)
    def _():
        h_scr[...] = _rms(x_ref[...], g_ref[...]).astype(BF16)
        acc_scr[...] = jnp.zeros_like(acc_scr)

    a = jnp.maximum(_dot(h_scr[...], w1_ref[...]), 0.0)
    acc_scr[...] += _dot((a * a).astype(BF16), w2_ref[...])

    @pl.when(f == pl.num_programs(1) - 1)
    def _():
        y = x_ref[...] + acc_scr[...]
        o_ref[...] = _rms(y, fg_ref[...]) if final_norm else y


def mlp(x, gain, w1, w2, final_gain, final_norm, tm=1024, tf=1024):
    n, d = x.shape
    d_ff = w1.shape[1]
    return pl.pallas_call(
        functools.partial(_mlp_body, final_norm=final_norm),
        grid=(n // tm, d_ff // tf),
        in_specs=[pl.BlockSpec((tm, d), lambda i, f: (i, 0)),
                  pl.BlockSpec((1, d), lambda i, f: (0, 0)),
                  pl.BlockSpec((d, tf), lambda i, f: (0, f)),
                  pl.BlockSpec((tf, d), lambda i, f: (f, 0)),
                  pl.BlockSpec((1, d), lambda i, f: (0, 0))],
        out_specs=pl.BlockSpec((tm, d), lambda i, f: (i, 0)),
        out_shape=jax.ShapeDtypeStruct((n, d), F32),
        scratch_shapes=[pltpu.VMEM((tm, d), BF16), pltpu.VMEM((tm, d), F32)],
        compiler_params=_params("parallel", "arbitrary"),
        name="mlp",
    )(x, gain.reshape(1, d), w1, w2, final_gain.reshape(1, d))


def _pad_heads(w, n_heads, dh):
    d = w.shape[0]
    w = w.reshape(d, n_heads, dh)
    return jnp.pad(w, ((0, 0), (0, 0), (0, LANES - dh))).reshape(d, n_heads * LANES)


def _even_layer(x2, seq_len, gain, w_in, b_i, b_f, head_gain, w_out):
    ml_qk = MLSTM_HEADS * MLSTM_DQK
    ml_v = MLSTM_HEADS * MLSTM_DV
    sb_w = SB_HEADS * SB_DH
    sizes = (ml_qk, ml_qk, ml_v, ml_v, MLSTM_HEADS, MLSTM_HEADS, sb_w, sb_w, sb_w)
    offs = [0]
    for s in sizes:
        offs.append(offs[-1] + s)
    part = [w_in[:, offs[i]:offs[i + 1]] for i in range(len(sizes))]
    mq, mk, mv, mo, mi, mf, sq, sk, sv = part
    w_main = jnp.concatenate([
        _pad_heads(mq, MLSTM_HEADS, MLSTM_DQK),
        _pad_heads(mk * (MLSTM_DQK ** -0.5), MLSTM_HEADS, MLSTM_DQK),
        mv, sq, sk, sv], axis=1).astype(BF16)
    w_gate_t = jnp.concatenate([mi, mf], axis=1).T.astype(BF16)
    proj, og, gates = norm_proj(x2, gain, [w_main, mo.astype(BF16)], [BF16, F32], gate_w_t=w_gate_t, seq_len=seq_len)
    bsz = x2.shape[0] // seq_len
    proj3 = proj.reshape(bsz, seq_len, proj.shape[1])
    bias = jnp.concatenate([b_i, b_f]).reshape(2 * MLSTM_HEADS, 1)
    hm = mlstm(proj3, og.reshape(bsz, seq_len, ml_v), gates, bias, head_gain.reshape(1, ml_v), 0, 1, 2)
    hs = stick_breaking(proj3, 3 * SB_HEADS, 4 * SB_HEADS, 5 * SB_HEADS)
    w_out = w_out.astype(BF16)
    return proj_res(x2, [hm.reshape(-1, ml_v), hs.reshape(-1, sb_w)], [w_out[:ml_v], w_out[ml_v:]])


def _odd_layer(x2, seq_len, gain, w_in, conv_w, a_log, dt_bias, head_gain, w_out):
    conv_ch = 3 * GDN_HEADS * GDN_DK
    vw = GDN_HEADS * GDN_DV
    w_main = w_in[:, :conv_ch + vw].astype(BF16)
    w_gate_t = w_in[:, conv_ch + vw:].T.astype(BF16)
    proj, gates = norm_proj(x2, gain, [w_main], [F32], gate_w_t=w_gate_t, seq_len=seq_len)
    bsz = x2.shape[0] // seq_len
    o = gated_deltanet(proj.reshape(bsz, seq_len, conv_ch + vw), gates, conv_w,
                       a_log.reshape(GDN_HEADS, 1), dt_bias.reshape(GDN_HEADS, 1), head_gain.reshape(1, GDN_DV))
    return proj_res(x2, [o.reshape(-1, vw)], [w_out.astype(BF16)])


def kernel(x, mem, mix_norm, ab_w_in, ab_b_i, ab_b_f, ab_head_gain, ab_w_out, c_w_in, c_conv_w, c_a_log, c_dt_bias, c_head_gain, c_w_out, xa_norm, mem_norm, xa_wq, xa_wk, xa_wv, xa_wo, mlp_norm, mlp_w1, mlp_w2, final_norm):
    bsz, seq_len, d = x.shape
    mem_len = mem.shape[1]
    depth = mix_norm.shape[0]
    x2 = x.reshape(bsz * seq_len, d)
    mem2 = mem.reshape(bsz * mem_len, d)
    for l in range(depth):
        j = l // 2
        if l % 2 == 0:
            x2 = _even_layer(x2, seq_len, mix_norm[l], ab_w_in[j], ab_b_i[j], ab_b_f[j], ab_head_gain[j], ab_w_out[j])
        else:
            x2 = _odd_layer(x2, seq_len, mix_norm[l], c_w_in[j], c_conv_w[j], c_a_log[j], c_dt_bias[j], c_head_gain[j], c_w_out[j])
        mem_k, mem_v = norm_proj(mem2, mem_norm, [xa_wk[l].astype(BF16), xa_wv[l].astype(BF16)], [BF16, BF16], tm=mem_len)
        x2 = xattn(x2, xa_norm[l], xa_wq[l].astype(BF16), mem_k.reshape(bsz, mem_len, d), mem_v.reshape(bsz, mem_len, d),
                   xa_wo[l].astype(BF16), seq_len)
        x2 = mlp(x2, mlp_norm[l], mlp_w1[l].astype(BF16), mlp_w2[l].astype(BF16), final_norm, l == depth - 1)
    return x2.reshape(bsz, seq_len, d)
```

```python
import functools

import jax
import jax.numpy as jnp
from jax import lax
from jax.experimental import pallas as pl
from jax.experimental.pallas import tpu as pltpu

F32 = jnp.float32
BF16 = jnp.bfloat16

EPS = 1e-6
GATE_SOFTCAP = 15.0

MLSTM_HEADS = 4
MLSTM_DQK = 64
MLSTM_DV = 128
SB_HEADS = 4
SB_DH = 128
GDN_HEADS = 8
GDN_DK = 128
GDN_DV = 128
GDN_CONV = 4
XA_HEADS = 4

LANES = 128
VMEM_LIMIT_BYTES = 48 * 1024 * 1024

NT_DIMS = (((1,), (1,)), ((), ()))
TN_DIMS = (((0,), (0,)), ((), ()))


def _params(*semantics):
    return pltpu.CompilerParams(dimension_semantics=semantics, vmem_limit_bytes=VMEM_LIMIT_BYTES)


def _dot(a, b):
    return jnp.dot(a, b, preferred_element_type=F32)


def _dot_nt(a, b):
    return lax.dot_general(a, b, NT_DIMS, preferred_element_type=F32)


def _dot_tn(a, b):
    return lax.dot_general(a, b, TN_DIMS, preferred_element_type=F32)


def _rms(x, g):
    return x * lax.rsqrt(jnp.mean(x * x, axis=-1, keepdims=True) + EPS) * g


def _sigmoid(x):
    return 1.0 / (1.0 + jnp.exp(-x))


def _softplus(x):
    return jnp.maximum(x, 0.0) + jnp.log(1.0 + jnp.exp(-jnp.abs(x)))


def _chunk_cumsum(x, chunk):
    pos = lax.broadcasted_iota(jnp.int32, x.shape, 1) % chunk
    shift = 1
    while shift < chunk:
        x = x + jnp.where(pos >= shift, pltpu.roll(x, shift, axis=1), 0.0)
        shift *= 2
    return x


def _col_from_row(row, eye):
    n = row.shape[1]
    return jnp.sum(jnp.where(eye, jnp.broadcast_to(row, (n, n)), 0.0), axis=1, keepdims=True)


PROJ_COL_CHUNK = 1024


def _norm_proj_body(*refs, n_out, has_gate):
    x_ref, g_ref = refs[0], refs[1]
    w_refs = refs[2:2 + n_out]
    pos = 2 + n_out
    wg_ref = refs[pos] if has_gate else None
    pos += int(has_gate)
    out_refs = refs[pos:pos + n_out]
    pos += n_out
    h = _rms(x_ref[...], g_ref[...]).astype(BF16)
    for w_ref, o_ref in zip(w_refs, out_refs):
        width = w_ref.shape[1]
        for c0 in range(0, width, PROJ_COL_CHUNK):
            c1 = min(c0 + PROJ_COL_CHUNK, width)
            o_ref[:, c0:c1] = _dot(h, w_ref[:, c0:c1]).astype(o_ref.dtype)
    if has_gate:
        refs[pos][0] = _dot_nt(wg_ref[...], h)


def norm_proj(x, gain, weights, out_dtypes, gate_w_t=None, seq_len=None, tm=512):
    n, d = x.shape
    has_gate = gate_w_t is not None
    in_specs = [pl.BlockSpec((tm, d), lambda i: (i, 0)), pl.BlockSpec((1, d), lambda i: (0, 0))]
    in_specs += [pl.BlockSpec(w.shape, lambda i: (0, 0)) for w in weights]
    args = [x, gain.reshape(1, d)] + list(weights)
    out_shape = [jax.ShapeDtypeStruct((n, w.shape[1]), dt) for w, dt in zip(weights, out_dtypes)]
    out_specs = [pl.BlockSpec((tm, w.shape[1]), lambda i: (i, 0)) for w in weights]
    if has_gate:
        n_gates = gate_w_t.shape[0]
        tiles_per_seq = seq_len // tm
        in_specs.append(pl.BlockSpec(gate_w_t.shape, lambda i: (0, 0)))
        args.append(gate_w_t)
        out_shape.append(jax.ShapeDtypeStruct((n // seq_len, n_gates, seq_len), F32))
        out_specs.append(pl.BlockSpec((1, n_gates, tm), lambda i: (i // tiles_per_seq, 0, i % tiles_per_seq)))
    return pl.pallas_call(
        functools.partial(_norm_proj_body, n_out=len(weights), has_gate=has_gate),
        grid=(n // tm,),
        in_specs=in_specs,
        out_specs=out_specs,
        out_shape=out_shape,
        compiler_params=_params("parallel"),
        name="norm_proj",
    )(*args)


def _proj_res_body(*refs, n_in):
    x_ref = refs[0]
    a_refs = refs[1:1 + n_in]
    w_refs = refs[1 + n_in:1 + 2 * n_in]
    o_ref = refs[1 + 2 * n_in]
    acc = x_ref[...]
    for a_ref, w_ref in zip(a_refs, w_refs):
        acc = acc + _dot(a_ref[...], w_ref[...])
    o_ref[...] = acc


def proj_res(x, acts, weights, tm=512):
    n, d = x.shape
    in_specs = [pl.BlockSpec((tm, d), lambda i: (i, 0))]
    in_specs += [pl.BlockSpec((tm, a.shape[1]), lambda i: (i, 0)) for a in acts]
    in_specs += [pl.BlockSpec(w.shape, lambda i: (0, 0)) for w in weights]
    return pl.pallas_call(
        functools.partial(_proj_res_body, n_in=len(acts)),
        grid=(n // tm,),
        in_specs=in_specs,
        out_specs=pl.BlockSpec((tm, d), lambda i: (i, 0)),
        out_shape=jax.ShapeDtypeStruct((n, d), F32),
        compiler_params=_params("parallel"),
        name="proj_res",
    )(x, *acts, *weights)


MLSTM_BLOCK = 512
MLSTM_CC = 128


def _mlstm_body(q_ref, k_ref, v_ref, og_ref, gate_ref, bias_ref, gain_ref, o_ref, c_scr, m_scr, ab_scr):
    n_heads = MLSTM_HEADS
    cc = MLSTM_CC
    block = q_ref.shape[1]

    @pl.when(pl.program_id(1) == 0)
    def _():
        c_scr[...] = jnp.zeros_like(c_scr)
        m_scr[...] = jnp.zeros_like(m_scr)

    pre = gate_ref[0] + bias_ref[...]
    log_i = GATE_SOFTCAP * jnp.tanh(pre[:n_heads] / GATE_SOFTCAP)
    log_f = -_softplus(-pre[n_heads:])
    b = _chunk_cumsum(log_f, cc)
    ab_scr[:n_heads] = log_i - b
    ab_scr[n_heads:] = b

    r_idx = lax.broadcasted_iota(jnp.int32, (cc, cc), 0)
    c_idx = lax.broadcasted_iota(jnp.int32, (cc, cc), 1)
    eye = r_idx == c_idx
    causal = c_idx <= r_idx
    ones_col = (lax.broadcasted_iota(jnp.int32, (cc, LANES), 1) == 0).astype(BF16)

    heads = range(n_heads)
    lanes = [slice(h * LANES, (h + 1) * LANES) for h in heads]
    m_prev = [m_scr[h:h + 1, 0:1] for h in heads]
    c_aug = [c_scr[h] for h in heads]
    for ci in range(block // cc):
        rows = slice(ci * cc, (ci + 1) * cc)
        q = [q_ref[0, rows, lanes[h]] for h in heads]
        k = [k_ref[0, rows, lanes[h]] for h in heads]
        v_aug = [jnp.concatenate([v_ref[0, rows, lanes[h]], ones_col], axis=1) for h in heads]
        a_row = [ab_scr[h:h + 1, rows] for h in heads]
        a_col = [_col_from_row(a_row[h], eye) for h in heads]
        b_col = [_col_from_row(ab_scr[n_heads + h:n_heads + h + 1, rows], eye) for h in heads]
        d = [jnp.where(causal, jnp.broadcast_to(a_row[h], (cc, cc)), -jnp.inf) for h in heads]
        g_col = [jnp.maximum(jnp.max(d[h], axis=1, keepdims=True), m_prev[h]) for h in heads]
        qk = [_dot_nt(q[h], k[h]) for h in heads]
        inter = [_dot(q[h], c_aug[h].astype(BF16)) for h in heads]
        s = [(qk[h] * jnp.exp(d[h] - g_col[h])).astype(BF16) for h in heads]
        g_last = [g_col[h][cc - 1:cc] for h in heads]
        kv = [_dot_tn(k[h], (v_aug[h].astype(F32) * jnp.exp(a_col[h] - g_last[h])).astype(BF16)) for h in heads]
        numden = [_dot(s[h], v_aug[h]) + jnp.exp(m_prev[h] - g_col[h]) * inter[h] for h in heads]
        c_aug = [jnp.exp(m_prev[h] - g_last[h]) * c_aug[h] + kv[h] for h in heads]
        m_prev = [b_col[h][cc - 1:cc] + g_last[h] for h in heads]
        for h in heads:
            num = numden[h][:, :MLSTM_DV]
            den = numden[h][:, MLSTM_DV:MLSTM_DV + 1]
            hh = num / jnp.maximum(jnp.abs(den), jnp.exp(-(b_col[h] + g_col[h])))
            hn = _rms(hh, gain_ref[:, lanes[h]]) * _sigmoid(og_ref[0, rows, lanes[h]])
            o_ref[0, rows, lanes[h]] = hn.astype(o_ref.dtype)
    for h in heads:
        c_scr[h] = c_aug[h]
        m_scr[h:h + 1, :] = jnp.broadcast_to(m_prev[h], (1, LANES))


def mlstm(proj, og, gates, bias, gain, q_blk, k_blk, v_blk):
    bsz, seq, _ = proj.shape
    width = MLSTM_HEADS * LANES
    blk = MLSTM_BLOCK

    def col(cb):
        return pl.BlockSpec((1, blk, width), lambda b, t: (b, t, cb))

    return pl.pallas_call(
        _mlstm_body,
        grid=(bsz, seq // blk),
        in_specs=[col(q_blk), col(k_blk), col(v_blk),
                  pl.BlockSpec((1, blk, width), lambda b, t: (b, t, 0)),
                  pl.BlockSpec((1, 2 * MLSTM_HEADS, blk), lambda b, t: (b, 0, t)),
                  pl.BlockSpec((2 * MLSTM_HEADS, 1), lambda b, t: (0, 0)),
                  pl.BlockSpec((1, width), lambda b, t: (0, 0))],
        out_specs=pl.BlockSpec((1, blk, width), lambda b, t: (b, t, 0)),
        out_shape=jax.ShapeDtypeStruct((bsz, seq, width), BF16),
        scratch_shapes=[pltpu.VMEM((MLSTM_HEADS, LANES, 2 * LANES), F32),
                        pltpu.VMEM((8, LANES), F32),
                        pltpu.VMEM((2 * MLSTM_HEADS, blk), F32)],
        compiler_params=_params("parallel", "arbitrary"),
        name="mlstm",
    )(proj, proj, proj, og, gates, bias, gain)


SB_TQ = 1024
SB_KT = 256
SB_KB = 512
SIGN_BIT = 0x80000000
SB_Q_SCALE = -(SB_DH ** -0.5) * 1.4426950408889634


def _sb_body(q_ref, k_ref, v_ref, o_ref, acc_scr, run_scr):
    tq = q_ref.shape[1]
    kt = SB_KT
    i = pl.program_id(2)
    acc_scr[...] = jnp.zeros_like(acc_scr)
    run_scr[...] = jnp.zeros_like(run_scr)

    suffix = (lax.broadcasted_iota(jnp.int32, (kt, kt), 0) > lax.broadcasted_iota(jnp.int32, (kt, kt), 1)).astype(BF16)

    def block(r0, k_start, n_tiles, diag_off):
        rows = tq - r0
        width = n_tiles * kt
        keys = pl.ds(pl.multiple_of(k_start, kt), width)
        zn = _dot_nt(q_ref[0, r0:, :], k_ref[0, keys, :])
        neg_abs = lax.bitcast_convert_type(lax.bitcast_convert_type(zn, jnp.uint32) | jnp.uint32(SIGN_BIT), F32)
        sp = jnp.log2(1.0 + jnp.exp2(neg_abs))
        log_1m = jnp.minimum(zn, 0.0) - sp
        log_beta = log_1m - zn
        mask = None
        if diag_off is not None:
            q_pos = lax.broadcasted_iota(jnp.int32, (rows, width), 0) + r0
            k_pos = lax.broadcasted_iota(jnp.int32, (rows, width), 1) + diag_off
            mask = k_pos < q_pos
            log_1m = jnp.where(mask, log_1m, 0.0)
        run = run_scr[r0:, :]
        weights = [None] * n_tiles
        for t in reversed(range(n_tiles)):
            cols = slice(t * kt, (t + 1) * kt)
            sums = _dot(log_1m[:, cols].astype(BF16), suffix)
            w = jnp.exp2(log_beta[:, cols] + sums + jnp.concatenate([run] * (kt // LANES), axis=1))
            if mask is not None:
                w = jnp.where(mask[:, cols], w, 0.0)
            weights[t] = w.astype(BF16)
            run = run + jnp.sum(log_1m[:, cols], axis=-1, keepdims=True)
        acc_scr[r0:, :] += _dot(jnp.concatenate(weights, axis=1), v_ref[0, keys, :])
        run_scr[r0:, :] = run

    for t in reversed(range(tq // kt)):
        block(t * kt, i * tq + t * kt, 1, t * kt)

    n_blocks = i * (tq // SB_KB)

    def body(step, carry):
        block(0, (n_blocks - 1 - step) * SB_KB, SB_KB // kt, None)
        return carry

    lax.fori_loop(0, n_blocks, body, 0)
    o_ref[0] = acc_scr[...].astype(o_ref.dtype)


def stick_breaking(proj, q_blk, k_blk, v_blk):
    bsz, seq, _ = proj.shape
    tq = SB_TQ
    return pl.pallas_call(
        _sb_body,
        grid=(bsz, SB_HEADS, seq // tq),
        in_specs=[pl.BlockSpec((1, tq, SB_DH), lambda b, h, i: (b, i, q_blk + h)),
                  pl.BlockSpec((1, seq, SB_DH), lambda b, h, i: (b, 0, k_blk + h)),
                  pl.BlockSpec((1, seq, SB_DH), lambda b, h, i: (b, 0, v_blk + h))],
        out_specs=pl.BlockSpec((1, tq, SB_DH), lambda b, h, i: (b, i, h)),
        out_shape=jax.ShapeDtypeStruct((bsz, seq, SB_HEADS * SB_DH), BF16),
        scratch_shapes=[pltpu.VMEM((tq, SB_DH), F32), pltpu.VMEM((tq, LANES), F32)],
        compiler_params=_params("parallel", "parallel", "arbitrary"),
        name="stickbreak",
    )(proj, proj, proj)


GDN_BLOCK = 512
GDN_CC = 128
GDN_GROUP = 2
CONV_PAD = 8
GDN_BASE = 8
GDN_BASE_STEPS = 2


def _gdn_body(q_ref, k_ref, v_ref, og_ref, gate_ref, cwq_ref, cwk_ref, cwv_ref, alog_ref, dtb_ref, gain_ref,
              o_ref, s_scr, ext_scr, u_scr, w_scr, att_scr, qd_scr, kdt_scr, dl_scr, oc_scr):
    block = q_ref.shape[1]
    cc = GDN_CC
    n_chunks = block // cc
    n_heads = GDN_HEADS
    group = GDN_GROUP
    width = group * LANES

    @pl.when(pl.program_id(2) == 0)
    def _():
        s_scr[...] = jnp.zeros_like(s_scr)
        ext_scr[...] = jnp.zeros_like(ext_scr)

    pre = gate_ref[0]
    beta_all = _sigmoid(pre[:n_heads])
    log_alpha = -jnp.exp(alog_ref[...]) * _softplus(pre[n_heads:] + dtb_ref[...])
    g_all = _chunk_cumsum(log_alpha, cc)

    def conv_silu(idx, x_ref, cw_ref):
        x = x_ref[0].astype(F32)
        ext = jnp.concatenate([ext_scr[idx], x], axis=0)
        y = x * cw_ref[GDN_CONV - 1:GDN_CONV, :]
        for lag in range(1, GDN_CONV):
            y = y + pltpu.roll(ext, lag, axis=0)[CONV_PAD:] * cw_ref[GDN_CONV - 1 - lag:GDN_CONV - lag, :]
        ext_scr[idx] = x[block - CONV_PAD:, :]
        return y * _sigmoid(y)

    def l2n(x):
        return x * lax.rsqrt(jnp.sum(x * x, axis=-1, keepdims=True) + EPS)

    yq = conv_silu(0, q_ref, cwq_ref)
    yk = conv_silu(1, k_ref, cwk_ref)
    yv = conv_silu(2, v_ref, cwv_ref)

    r_idx = lax.broadcasted_iota(jnp.int32, (cc, cc), 0)
    c_idx = lax.broadcasted_iota(jnp.int32, (cc, cc), 1)
    eye = r_idx == c_idx
    incl = c_idx <= r_idx
    strict = c_idx < r_idx
    eye_f = eye.astype(F32)
    base_mask = (r_idx // GDN_BASE) == (c_idx // GDN_BASE)
    merge_masks = []
    size = GDN_BASE
    while size < cc:
        merge_masks.append(((r_idx // (2 * size)) == (c_idx // (2 * size))) & ((r_idx // size) != (c_idx // size)))
        size *= 2
    head_idx =lax.broadcasted_iota(jnp.int32, (n_heads, block), 0)

    problems = [(hh, c) for hh in range(group) for c in range(n_chunks)]
    qkv = []
    for hh in range(group):
        lanes = slice(hh * LANES, (hh + 1) * LANES)
        is_head = head_idx == pl.program_id(1) * group + hh
        g_head = jnp.sum(jnp.where(is_head, g_all, 0.0), axis=0, keepdims=True)
        beta_head = jnp.sum(jnp.where(is_head, beta_all, 0.0), axis=0, keepdims=True)
        qkv.append((l2n(yq[:, lanes]) * (GDN_DK ** -0.5), l2n(yk[:, lanes]), yv[:, lanes], g_head, beta_head))

    pre_state = []
    for hh, c in problems:
        rows = slice(c * cc, (c + 1) * cc)
        q_h, k_h, v_h, g_head, beta_head = qkv[hh]
        q = q_h[rows]
        k = k_h[rows]
        g_row = g_head[:, rows]
        g_col = _col_from_row(g_row, eye)
        beta_col = _col_from_row(beta_head[:, rows], eye)
        decay = jnp.where(incl, jnp.exp(jnp.where(incl, g_col - g_row, 0.0)), 0.0)
        k_b = k.astype(BF16)
        kk = _dot_nt(k_b, k_b)
        qk = _dot_nt(q.astype(BF16), k_b)
        exp_g = jnp.exp(g_col)
        g_last = g_col[cc - 1:cc]
        att_scr[hh, rows] = (qk * decay).astype(BF16)
        qd_scr[hh, rows] = (q * exp_g).astype(BF16)
        kdt_scr[hh, :, rows] = (k * jnp.exp(g_last - g_col)).T.astype(BF16)
        dl_scr[hh, c] = jnp.broadcast_to(jnp.exp(g_last), (8, LANES))
        rhs = jnp.concatenate([v_h[rows] * beta_col, k * (beta_col * exp_g)], axis=1)
        pre_state.append((jnp.where(strict, beta_col * kk * decay, 0.0), rhs))

    low = [m for m, _ in pre_state]
    powers = [jnp.where(base_mask, -m, 0.0) for m in low]
    n_accs = list(powers)
    for _ in range(GDN_BASE_STEPS):
        powers = [_dot(p.astype(BF16), p.astype(BF16)) for p in powers]
        n_accs = [n + p + _dot(n.astype(BF16), p.astype(BF16)) for n, p in zip(n_accs, powers)]
    invs = [eye_f + n for n in n_accs]
    for off_mask in merge_masks:
        inv_b = [t.astype(BF16) for t in invs]
        corr = [_dot(jnp.where(off_mask, m, 0.0).astype(BF16), t) for m, t in zip(low, inv_b)]
        invs = [t - _dot(t_b, e.astype(BF16)) for t, t_b, e in zip(invs, inv_b, corr)]
    for (hh, c), inv, (_, rhs) in zip(problems, invs, pre_state):
        rows = slice(c * cc, (c + 1) * cc)
        sol = rhs + _dot((inv - eye_f).astype(BF16), rhs.astype(BF16))
        u_scr[hh, rows] = sol[:, :GDN_DV]
        w_scr[hh, rows] = sol[:, GDN_DV:].astype(BF16)

    state = [s_scr[hh] for hh in range(group)]
    heads = range(group)
    for c in range(n_chunks):
        rows = slice(c * cc, (c + 1) * cc)
        s_b = [state[hh].astype(BF16) for hh in heads]
        v_b = [(u_scr[hh, rows] - _dot(w_scr[hh, rows], s_b[hh])).astype(BF16) for hh in heads]
        state = [state[hh] * dl_scr[hh, c, 0:1, :] + _dot(kdt_scr[hh, :, rows], v_b[hh]) for hh in heads]
        for hh in heads:
            oc_scr[rows, hh * LANES:(hh + 1) * LANES] = _dot(qd_scr[hh, rows], s_b[hh]) + _dot(att_scr[hh, rows], v_b[hh])
    for hh in heads:
        s_scr[hh] = state[hh]

    for hh in range(group):
        lanes = slice(hh * LANES, (hh + 1) * LANES)
        og = og_ref[0, :, lanes].astype(F32)
        o_ref[0, :, lanes] = (_rms(oc_scr[:, lanes], gain_ref[...]) * (og * _sigmoid(og))).astype(o_ref.dtype)


def gated_deltanet(proj, gates, conv_w, a_log, dt_bias, gain):
    bsz, seq, _ = proj.shape
    n_heads = GDN_HEADS
    blk = GDN_BLOCK

    grp = GDN_GROUP
    n_groups = n_heads // grp
    width = grp * LANES

    def col(part):
        return pl.BlockSpec((1, blk, width), lambda b, h, t: (b, t, part * n_groups + h))

    def conv_spec(part):
        return pl.BlockSpec((GDN_CONV, width), lambda b, h, t: (0, part * n_groups + h))

    small = pl.BlockSpec((n_heads, 1), lambda b, h, t: (0, 0))
    return pl.pallas_call(
        _gdn_body,
        grid=(bsz, n_groups, seq // blk),
        in_specs=[col(0), col(1), col(2), col(3),
                  pl.BlockSpec((1, 2 * n_heads, blk), lambda b, h, t: (b, 0, t)),
                  conv_spec(0), conv_spec(1), conv_spec(2), small, small,
                  pl.BlockSpec((1, GDN_DV), lambda b, h, t: (0, 0))],
        out_specs=pl.BlockSpec((1, blk, width), lambda b, h, t: (b, t, h)),
        out_shape=jax.ShapeDtypeStruct((bsz, seq, n_heads * GDN_DV), BF16),
        scratch_shapes=[pltpu.VMEM((grp, GDN_DK, GDN_DV), F32),
                        pltpu.VMEM((3, CONV_PAD, width), F32),
                        pltpu.VMEM((grp, blk, GDN_DV), F32),
                        pltpu.VMEM((grp, blk, GDN_DK), BF16),
                        pltpu.VMEM((grp, blk, GDN_CC), BF16),
                        pltpu.VMEM((grp, blk, GDN_DK), BF16),
                        pltpu.VMEM((grp, GDN_DK, blk), BF16),
                        pltpu.VMEM((grp, blk // GDN_CC, 8, LANES), F32),
                        pltpu.VMEM((blk, width), F32)],
        compiler_params=_params("parallel", "parallel", "arbitrary"),
        name="gdn",
    )(proj, proj, proj, proj, gates, conv_w, conv_w, conv_w, a_log, dt_bias, gain)


def _xattn_body(x_ref, g_ref, wq_ref, k_ref, v_ref, wo_ref, o_ref, *, n_heads):
    x = x_ref[...]
    d = x.shape[1]
    dh = d // n_heads
    h = _rms(x, g_ref[...]).astype(BF16)
    q = _dot(h, wq_ref[...]).astype(BF16)
    outs = []
    for hd in range(n_heads):
        lanes = slice(hd * dh, (hd + 1) * dh)
        s = _dot_nt(q[:, lanes], k_ref[0, :, lanes]) * (dh ** -0.5)
        e = jnp.exp(s - jnp.max(s, axis=-1, keepdims=True))
        p = e / jnp.sum(e, axis=-1, keepdims=True)
        outs.append(_dot(p.astype(BF16), v_ref[0, :, lanes]).astype(BF16))
    o_ref[...] = x + _dot(jnp.concatenate(outs, axis=1), wo_ref[...])


def xattn(x, gain, wq, k, v, wo, seq_len, tm=512):
    n, d = x.shape
    mem_len = k.shape[1]
    tiles_per_seq = seq_len // tm
    return pl.pallas_call(
        functools.partial(_xattn_body, n_heads=XA_HEADS),
        grid=(n // tm,),
        in_specs=[pl.BlockSpec((tm, d), lambda i: (i, 0)),
                  pl.BlockSpec((1, d), lambda i: (0, 0)),
                  pl.BlockSpec((d, d), lambda i: (0, 0)),
                  pl.BlockSpec((1, mem_len, d), lambda i: (i // tiles_per_seq, 0, 0)),
                  pl.BlockSpec((1, mem_len, d), lambda i: (i // tiles_per_seq, 0, 0)),
                  pl.BlockSpec((d, d), lambda i: (0, 0))],
        out_specs=pl.BlockSpec((tm, d), lambda i: (i, 0)),
        out_shape=jax.ShapeDtypeStruct((n, d), F32),
        compiler_params=_params("parallel"),
        name="xattn",
    )(x, gain.reshape(1, d), wq, k, v, wo)


def _mlp_body(x_ref, g_ref, w1_ref, w2_ref, fg_ref, o_ref, h_scr, acc_scr, *, final_norm):
    f = pl.program_id(1)

    @pl.when(f == 0)
    def _():
        h_scr[...] = _rms(x_ref[...], g_ref[...]).astype(BF16)
        acc_scr[...] = jnp.zeros_like(acc_scr)

    a = jnp.maximum(_dot(h_scr[...], w1_ref[...]), 0.0)
    acc_scr[...] += _dot((a * a).astype(BF16), w2_ref[...])

    @pl.when(f == pl.num_programs(1) - 1)
    def _():
        y = x_ref[...] + acc_scr[...]
        o_ref[...] = _rms(y, fg_ref[...]) if final_norm else y


def mlp(x, gain, w1, w2, final_gain, final_norm, tm=1024, tf=1024):
    n, d = x.shape
    d_ff = w1.shape[1]
    return pl.pallas_call(
        functools.partial(_mlp_body, final_norm=final_norm),
        grid=(n // tm, d_ff // tf),
        in_specs=[pl.BlockSpec((tm, d), lambda i, f: (i, 0)),
                  pl.BlockSpec((1, d), lambda i, f: (0, 0)),
                  pl.BlockSpec((d, tf), lambda i, f: (0, f)),
                  pl.BlockSpec((tf, d), lambda i, f: (f, 0)),
                  pl.BlockSpec((1, d), lambda i, f: (0, 0))],
        out_specs=pl.BlockSpec((tm, d), lambda i, f: (i, 0)),
        out_shape=jax.ShapeDtypeStruct((n, d), F32),
        scratch_shapes=[pltpu.VMEM((tm, d), BF16), pltpu.VMEM((tm, d), F32)],
        compiler_params=_params("parallel", "arbitrary"),
        name="mlp",
    )(x, gain.reshape(1, d), w1, w2, final_gain.reshape(1, d))


def _pad_heads(w, n_heads, dh):
    d = w.shape[0]
    w = w.reshape(d, n_heads, dh)
    return jnp.pad(w, ((0, 0), (0, 0), (0, LANES - dh))).reshape(d, n_heads * LANES)


def _even_layer(x2, seq_len, gain, w_in, b_i, b_f, head_gain, w_out):
    ml_qk = MLSTM_HEADS * MLSTM_DQK
    ml_v = MLSTM_HEADS * MLSTM_DV
    sb_w = SB_HEADS * SB_DH
    sizes = (ml_qk, ml_qk, ml_v, ml_v, MLSTM_HEADS, MLSTM_HEADS, sb_w, sb_w, sb_w)
    offs = [0]
    for s in sizes:
        offs.append(offs[-1] + s)
    part = [w_in[:, offs[i]:offs[i + 1]] for i in range(len(sizes))]
    mq, mk, mv, mo, mi, mf, sq, sk, sv = part
    w_main = jnp.concatenate([
        _pad_heads(mq, MLSTM_HEADS, MLSTM_DQK),
        _pad_heads(mk * (MLSTM_DQK ** -0.5), MLSTM_HEADS, MLSTM_DQK),
        mv, sq * SB_Q_SCALE, sk, sv], axis=1).astype(BF16)
    w_gate_t = jnp.concatenate([mi, mf], axis=1).T.astype(BF16)
    proj, og, gates = norm_proj(x2, gain, [w_main, mo.astype(BF16)], [BF16, F32], gate_w_t=w_gate_t, seq_len=seq_len)
    bsz = x2.shape[0] // seq_len
    proj3 = proj.reshape(bsz, seq_len, proj.shape[1])
    bias = jnp.concatenate([b_i, b_f]).reshape(2 * MLSTM_HEADS, 1)
    hm = mlstm(proj3, og.reshape(bsz, seq_len, ml_v), gates, bias, head_gain.reshape(1, ml_v), 0, 1, 2)
    hs = stick_breaking(proj3, 3 * SB_HEADS, 4 * SB_HEADS, 5 * SB_HEADS)
    w_out = w_out.astype(BF16)
    return proj_res(x2, [hm.reshape(-1, ml_v), hs.reshape(-1, sb_w)], [w_out[:ml_v], w_out[ml_v:]])


def _odd_layer(x2, seq_len, gain, w_in, conv_w, a_log, dt_bias, head_gain, w_out):
    conv_ch = 3 * GDN_HEADS * GDN_DK
    vw = GDN_HEADS * GDN_DV
    w_main = w_in[:, :conv_ch + vw].astype(BF16)
    w_gate_t = w_in[:, conv_ch + vw:].T.astype(BF16)
    proj, gates = norm_proj(x2, gain, [w_main], [BF16], gate_w_t=w_gate_t, seq_len=seq_len)
    bsz = x2.shape[0] // seq_len
    o = gated_deltanet(proj.reshape(bsz, seq_len, conv_ch + vw), gates, conv_w,
                       a_log.reshape(GDN_HEADS, 1), dt_bias.reshape(GDN_HEADS, 1), head_gain.reshape(1, GDN_DV))
    return proj_res(x2, [o.reshape(-1, vw)], [w_out.astype(BF16)])


def kernel(x, mem, mix_norm, ab_w_in, ab_b_i, ab_b_f, ab_head_gain, ab_w_out, c_w_in, c_conv_w, c_a_log, c_dt_bias, c_head_gain, c_w_out, xa_norm, mem_norm, xa_wq, xa_wk, xa_wv, xa_wo, mlp_norm, mlp_w1, mlp_w2, final_norm):
    bsz, seq_len, d = x.shape
    mem_len = mem.shape[1]
    depth = mix_norm.shape[0]
    x2 = x.reshape(bsz * seq_len, d)
    mem2 = mem.reshape(bsz * mem_len, d)
    for l in range(depth):
        j = l // 2
        if l % 2 == 0:
            x2 = _even_layer(x2, seq_len, mix_norm[l], ab_w_in[j], ab_b_i[j], ab_b_f[j], ab_head_gain[j], ab_w_out[j])
        else:
            x2 = _odd_layer(x2, seq_len, mix_norm[l], c_w_in[j], c_conv_w[j], c_a_log[j], c_dt_bias[j], c_head_gain[j], c_w_out[j])
        mem_k, mem_v = norm_proj(mem2, mem_norm, [xa_wk[l].astype(BF16), xa_wv[l].astype(BF16)], [BF16, BF16], tm=mem_len)
        x2 = xattn(x2, xa_norm[l], xa_wq[l].astype(BF16), mem_k.reshape(bsz, mem_len, d), mem_v.reshape(bsz, mem_len, d),
                   xa_wo[l].astype(BF16), seq_len)
        x2 = mlp(x2, mlp_norm[l], mlp_w1[l].astype(BF16), mlp_w2[l].astype(BF16), final_norm, l == depth - 1)
    return x2.reshape(bsz, seq_len, d)
```

```python
import functools

import jax
import jax.numpy as jnp
from jax import lax
from jax.experimental import pallas as pl
from jax.experimental.pallas import tpu as pltpu

F32 = jnp.float32
BF16 = jnp.bfloat16

EPS = 1e-6
GATE_SOFTCAP = 15.0

MLSTM_HEADS = 4
MLSTM_DQK = 64
MLSTM_DV = 128
SB_HEADS = 4
SB_DH = 128
GDN_HEADS = 8
GDN_DK = 128
GDN_DV = 128
GDN_CONV = 4
XA_HEADS = 4

LANES = 128
VMEM_LIMIT_BYTES = 48 * 1024 * 1024

NT_DIMS = (((1,), (1,)), ((), ()))
TN_DIMS = (((0,), (0,)), ((), ()))


def _params(*semantics):
    return pltpu.CompilerParams(dimension_semantics=semantics, vmem_limit_bytes=VMEM_LIMIT_BYTES)


def _dot(a, b):
    return jnp.dot(a, b, preferred_element_type=F32)


def _dot_nt(a, b):
    return lax.dot_general(a, b, NT_DIMS, preferred_element_type=F32)


def _dot_tn(a, b):
    return lax.dot_general(a, b, TN_DIMS, preferred_element_type=F32)


def _rms(x, g):
    return x * lax.rsqrt(jnp.mean(x * x, axis=-1, keepdims=True) + EPS) * g


def _sigmoid(x):
    return 1.0 / (1.0 + jnp.exp(-x))


def _softplus(x):
    return jnp.maximum(x, 0.0) + jnp.log(1.0 + jnp.exp(-jnp.abs(x)))


def _chunk_cumsum(x, chunk):
    pos = lax.broadcasted_iota(jnp.int32, x.shape, 1) % chunk
    shift = 1
    while shift < chunk:
        x = x + jnp.where(pos >= shift, pltpu.roll(x, shift, axis=1), 0.0)
        shift *= 2
    return x


def _col_from_row(row, eye):
    n = row.shape[1]
    return jnp.sum(jnp.where(eye, jnp.broadcast_to(row, (n, n)), 0.0), axis=1, keepdims=True)


PROJ_COL_CHUNK = 1024


def _norm_proj_body(*refs, n_out, has_gate):
    x_ref, g_ref = refs[0], refs[1]
    w_refs = refs[2:2 + n_out]
    pos = 2 + n_out
    wg_ref = refs[pos] if has_gate else None
    pos += int(has_gate)
    out_refs = refs[pos:pos + n_out]
    pos += n_out
    h = _rms(x_ref[...], g_ref[...]).astype(BF16)
    for w_ref, o_ref in zip(w_refs, out_refs):
        width = w_ref.shape[1]
        for c0 in range(0, width, PROJ_COL_CHUNK):
            c1 = min(c0 + PROJ_COL_CHUNK, width)
            o_ref[:, c0:c1] = _dot(h, w_ref[:, c0:c1]).astype(o_ref.dtype)
    if has_gate:
        refs[pos][0] = _dot_nt(wg_ref[...], h)


def norm_proj(x, gain, weights, out_dtypes, gate_w_t=None, seq_len=None, tm=512):
    n, d = x.shape
    has_gate = gate_w_t is not None
    in_specs = [pl.BlockSpec((tm, d), lambda i: (i, 0)), pl.BlockSpec((1, d), lambda i: (0, 0))]
    in_specs += [pl.BlockSpec(w.shape, lambda i: (0, 0)) for w in weights]
    args = [x, gain.reshape(1, d)] + list(weights)
    out_shape = [jax.ShapeDtypeStruct((n, w.shape[1]), dt) for w, dt in zip(weights, out_dtypes)]
    out_specs = [pl.BlockSpec((tm, w.shape[1]), lambda i: (i, 0)) for w in weights]
    if has_gate:
        n_gates = gate_w_t.shape[0]
        tiles_per_seq = seq_len // tm
        in_specs.append(pl.BlockSpec(gate_w_t.shape, lambda i: (0, 0)))
        args.append(gate_w_t)
        out_shape.append(jax.ShapeDtypeStruct((n // seq_len, n_gates, seq_len), F32))
        out_specs.append(pl.BlockSpec((1, n_gates, tm), lambda i: (i // tiles_per_seq, 0, i % tiles_per_seq)))
    return pl.pallas_call(
        functools.partial(_norm_proj_body, n_out=len(weights), has_gate=has_gate),
        grid=(n // tm,),
        in_specs=in_specs,
        out_specs=out_specs,
        out_shape=out_shape,
        compiler_params=_params("parallel"),
        name="norm_proj",
    )(*args)


MLSTM_BLOCK = 512
MLSTM_CC = 128


def _mlstm_body(q_ref, k_ref, v_ref, og_ref, gate_ref, bias_ref, gain_ref, o_ref, c_scr, m_scr, ab_scr):
    n_heads = MLSTM_HEADS
    cc = MLSTM_CC
    block = q_ref.shape[1]

    @pl.when(pl.program_id(1) == 0)
    def _():
        c_scr[...] = jnp.zeros_like(c_scr)
        m_scr[...] = jnp.zeros_like(m_scr)

    pre = gate_ref[0] + bias_ref[...]
    log_i = GATE_SOFTCAP * jnp.tanh(pre[:n_heads] / GATE_SOFTCAP)
    log_f = -_softplus(-pre[n_heads:])
    b = _chunk_cumsum(log_f, cc)
    ab_scr[:n_heads] = log_i - b
    ab_scr[n_heads:] = b

    r_idx = lax.broadcasted_iota(jnp.int32, (cc, cc), 0)
    c_idx = lax.broadcasted_iota(jnp.int32, (cc, cc), 1)
    eye = r_idx == c_idx
    causal = c_idx <= r_idx
    ones_col = (lax.broadcasted_iota(jnp.int32, (cc, LANES), 1) == 0).astype(BF16)

    heads = range(n_heads)
    lanes = [slice(h * LANES, (h + 1) * LANES) for h in heads]
    m_prev = [m_scr[h:h + 1, 0:1] for h in heads]
    c_aug = [c_scr[h] for h in heads]
    for ci in range(block // cc):
        rows = slice(ci * cc, (ci + 1) * cc)
        q = [q_ref[0, rows, lanes[h]] for h in heads]
        k = [k_ref[0, rows, lanes[h]] for h in heads]
        v_aug = [jnp.concatenate([v_ref[0, rows, lanes[h]], ones_col], axis=1) for h in heads]
        a_row = [ab_scr[h:h + 1, rows] for h in heads]
        a_col = [_col_from_row(a_row[h], eye) for h in heads]
        b_col = [_col_from_row(ab_scr[n_heads + h:n_heads + h + 1, rows], eye) for h in heads]
        d = [jnp.where(causal, jnp.broadcast_to(a_row[h], (cc, cc)), -jnp.inf) for h in heads]
        g_col = [jnp.maximum(jnp.max(d[h], axis=1, keepdims=True), m_prev[h]) for h in heads]
        qk = [_dot_nt(q[h], k[h]) for h in heads]
        inter = [_dot(q[h], c_aug[h].astype(BF16)) for h in heads]
        s = [(qk[h] * jnp.exp(d[h] - g_col[h])).astype(BF16) for h in heads]
        g_last = [g_col[h][cc - 1:cc] for h in heads]
        kv = [_dot_tn(k[h], (v_aug[h].astype(F32) * jnp.exp(a_col[h] - g_last[h])).astype(BF16)) for h in heads]
        numden = [_dot(s[h], v_aug[h]) + jnp.exp(m_prev[h] - g_col[h]) * inter[h] for h in heads]
        c_aug = [jnp.exp(m_prev[h] - g_last[h]) * c_aug[h] + kv[h] for h in heads]
        m_prev = [b_col[h][cc - 1:cc] + g_last[h] for h in heads]
        for h in heads:
            num = numden[h][:, :MLSTM_DV]
            den = numden[h][:, MLSTM_DV:MLSTM_DV + 1]
            hh = num / jnp.maximum(jnp.abs(den), jnp.exp(-(b_col[h] + g_col[h])))
            hn = _rms(hh, gain_ref[:, lanes[h]]) * _sigmoid(og_ref[0, rows, lanes[h]])
            o_ref[0, rows, lanes[h]] = hn.astype(o_ref.dtype)
    for h in heads:
        c_scr[h] = c_aug[h]
        m_scr[h:h + 1, :] = jnp.broadcast_to(m_prev[h], (1, LANES))


def mlstm(proj, og, gates, bias, gain, q_blk, k_blk, v_blk):
    bsz, seq, _ = proj.shape
    width = MLSTM_HEADS * LANES
    blk = MLSTM_BLOCK

    def col(cb):
        return pl.BlockSpec((1, blk, width), lambda b, t: (b, t, cb))

    return pl.pallas_call(
        _mlstm_body,
        grid=(bsz, seq // blk),
        in_specs=[col(q_blk), col(k_blk), col(v_blk),
                  pl.BlockSpec((1, blk, width), lambda b, t: (b, t, 0)),
                  pl.BlockSpec((1, 2 * MLSTM_HEADS, blk), lambda b, t: (b, 0, t)),
                  pl.BlockSpec((2 * MLSTM_HEADS, 1), lambda b, t: (0, 0)),
                  pl.BlockSpec((1, width), lambda b, t: (0, 0))],
        out_specs=pl.BlockSpec((1, blk, width), lambda b, t: (b, t, 0)),
        out_shape=jax.ShapeDtypeStruct((bsz, seq, width), BF16),
        scratch_shapes=[pltpu.VMEM((MLSTM_HEADS, LANES, 2 * LANES), F32),
                        pltpu.VMEM((8, LANES), F32),
                        pltpu.VMEM((2 * MLSTM_HEADS, blk), F32)],
        compiler_params=_params("parallel", "arbitrary"),
        name="mlstm",
    )(proj, proj, proj, og, gates, bias, gain)


SB_TQ = 1024
SB_KT = 256
SB_KB = 512
SIGN_BIT = 0x80000000
SB_DEAD_LOG2 = -150.0
SB_Q_SCALE = -(SB_DH ** -0.5) * 1.4426950408889634


def _sb_body(q_ref, k_ref, v_ref, o_ref, acc_scr, run_scr):
    tq = q_ref.shape[1]
    kt = SB_KT
    i = pl.program_id(2)
    acc_scr[...] = jnp.zeros_like(acc_scr)
    run_scr[...] = jnp.zeros_like(run_scr)

    suffix = (lax.broadcasted_iota(jnp.int32, (kt, kt), 0) > lax.broadcasted_iota(jnp.int32, (kt, kt), 1)).astype(BF16)

    def block(r0, k_start, n_tiles, diag_off):
        rows = tq - r0
        width = n_tiles * kt
        keys = pl.ds(pl.multiple_of(k_start, kt), width)
        zn = _dot_nt(q_ref[0, r0:, :], k_ref[0, keys, :])
        neg_abs = lax.bitcast_convert_type(lax.bitcast_convert_type(zn, jnp.uint32) | jnp.uint32(SIGN_BIT), F32)
        sp = jnp.log2(1.0 + jnp.exp2(neg_abs))
        log_1m = jnp.minimum(zn, 0.0) - sp
        log_beta = log_1m - zn
        mask = None
        if diag_off is not None:
            q_pos = lax.broadcasted_iota(jnp.int32, (rows, width), 0) + r0
            k_pos = lax.broadcasted_iota(jnp.int32, (rows, width), 1) + diag_off
            mask = k_pos < q_pos
            log_1m = jnp.where(mask, log_1m, 0.0)
        run = run_scr[r0:, :]
        weights = [None] * n_tiles
        for t in reversed(range(n_tiles)):
            cols = slice(t * kt, (t + 1) * kt)
            sums = _dot(log_1m[:, cols].astype(BF16), suffix)
            w = jnp.exp2(log_beta[:, cols] + sums + jnp.concatenate([run] * (kt // LANES), axis=1))
            if mask is not None:
                w = jnp.where(mask[:, cols], w, 0.0)
            weights[t] = w.astype(BF16)
            run = run + jnp.sum(log_1m[:, cols], axis=-1, keepdims=True)
        acc_scr[r0:, :] += _dot(jnp.concatenate(weights, axis=1), v_ref[0, keys, :])
        run_scr[r0:, :] = run

    for t in reversed(range(tq // kt)):
        block(t * kt, i * tq + t * kt, 1, t * kt)

    n_blocks = i * (tq // SB_KB)

    def live(step):
        return jnp.logical_and(step < n_blocks, jnp.max(run_scr[...]) > SB_DEAD_LOG2)

    def body(step):
        block(0, (n_blocks - 1 - step) * SB_KB, SB_KB // kt, None)
        return step + 1

    lax.while_loop(live, body, 0)
    o_ref[0] = acc_scr[...].astype(o_ref.dtype)


def stick_breaking(proj, q_blk, k_blk, v_blk):
    bsz, seq, _ = proj.shape
    tq = SB_TQ
    return pl.pallas_call(
        _sb_body,
        grid=(bsz, SB_HEADS, seq // tq),
        in_specs=[pl.BlockSpec((1, tq, SB_DH), lambda b, h, i: (b, i, q_blk + h)),
                  pl.BlockSpec((1, seq, SB_DH), lambda b, h, i: (b, 0, k_blk + h)),
                  pl.BlockSpec((1, seq, SB_DH), lambda b, h, i: (b, 0, v_blk + h))],
        out_specs=pl.BlockSpec((1, tq, SB_DH), lambda b, h, i: (b, i, h)),
        out_shape=jax.ShapeDtypeStruct((bsz, seq, SB_HEADS * SB_DH), BF16),
        scratch_shapes=[pltpu.VMEM((tq, SB_DH), F32), pltpu.VMEM((tq, LANES), F32)],
        compiler_params=_params("parallel", "parallel", "arbitrary"),
        name="stickbreak",
    )(proj, proj, proj)


GDN_BLOCK = 512
GDN_CC = 128
GDN_GROUP = 4
CONV_PAD = 8
GDN_BASE = 8
GDN_BASE_STEPS = 2


def _gdn_body(q_ref, k_ref, v_ref, og_ref, gate_ref, cwq_ref, cwk_ref, cwv_ref, alog_ref, dtb_ref, gain_ref,
              o_ref, s_scr, ext_scr, u_scr, w_scr, att_scr, qd_scr, kdt_scr, dl_scr, oc_scr):
    block = q_ref.shape[1]
    cc = GDN_CC
    n_chunks = block // cc
    n_heads = GDN_HEADS
    group = GDN_GROUP
    width = group * LANES

    @pl.when(pl.program_id(2) == 0)
    def _():
        s_scr[...] = jnp.zeros_like(s_scr)
        ext_scr[...] = jnp.zeros_like(ext_scr)

    pre = gate_ref[0]
    beta_all = _sigmoid(pre[:n_heads])
    log_alpha = -jnp.exp(alog_ref[...]) * _softplus(pre[n_heads:] + dtb_ref[...])
    g_all = _chunk_cumsum(log_alpha, cc)

    sub_idx = lax.broadcasted_iota(jnp.int32, (block // CONV_PAD, CONV_PAD, width), 1)

    def conv_silu(idx, x_ref, cw_ref):
        x = x_ref[0].astype(F32)
        tiles = x.reshape(block // CONV_PAD, CONV_PAD, width)
        prev = ext_scr[idx].reshape(1, CONV_PAD, width)
        y = tiles * cw_ref[GDN_CONV - 1:GDN_CONV, :]
        for lag in range(1, GDN_CONV):
            rot = pltpu.roll(tiles, lag, axis=1)
            rot_before = jnp.concatenate([pltpu.roll(prev, lag, axis=1), rot[:-1]], axis=0)
            y = y + jnp.where(sub_idx >= lag, rot, rot_before) * cw_ref[GDN_CONV - 1 - lag:GDN_CONV - lag, :]
        ext_scr[idx] = x[block - CONV_PAD:, :]
        y = y.reshape(block, width)
        return y * _sigmoid(y)

    def l2n(x):
        return x * lax.rsqrt(jnp.sum(x * x, axis=-1, keepdims=True) + EPS)

    yq = conv_silu(0, q_ref, cwq_ref)
    yk = conv_silu(1, k_ref, cwk_ref)
    yv = conv_silu(2, v_ref, cwv_ref)

    r_idx = lax.broadcasted_iota(jnp.int32, (cc, cc), 0)
    c_idx = lax.broadcasted_iota(jnp.int32, (cc, cc), 1)
    eye = r_idx == c_idx
    incl = c_idx <= r_idx
    strict = c_idx < r_idx
    eye_f = eye.astype(F32)
    base_mask = (r_idx // GDN_BASE) == (c_idx // GDN_BASE)
    merge_masks = []
    size = GDN_BASE
    while size < cc:
        merge_masks.append(((r_idx // (2 * size)) == (c_idx // (2 * size))) & ((r_idx // size) != (c_idx // size)))
        size *= 2
    head_idx =lax.broadcasted_iota(jnp.int32, (n_heads, block), 0)

    problems = [(hh, c) for hh in range(group) for c in range(n_chunks)]
    qkv = []
    for hh in range(group):
        lanes = slice(hh * LANES, (hh + 1) * LANES)
        is_head = head_idx == pl.program_id(1) * group + hh
        g_head = jnp.sum(jnp.where(is_head, g_all, 0.0), axis=0, keepdims=True)
        beta_head = jnp.sum(jnp.where(is_head, beta_all, 0.0), axis=0, keepdims=True)
        qkv.append((l2n(yq[:, lanes]) * (GDN_DK ** -0.5), l2n(yk[:, lanes]), yv[:, lanes], g_head, beta_head))

    pre_state = []
    for hh, c in problems:
        rows = slice(c * cc, (c + 1) * cc)
        q_h, k_h, v_h, g_head, beta_head = qkv[hh]
        q = q_h[rows]
        k = k_h[rows]
        g_row = g_head[:, rows]
        g_col = _col_from_row(g_row, eye)
        beta_col = _col_from_row(beta_head[:, rows], eye)
        decay = jnp.where(incl, jnp.exp(jnp.where(incl, g_col - g_row, 0.0)), 0.0)
        k_b = k.astype(BF16)
        kk = _dot_nt(k_b, k_b)
        qk = _dot_nt(q.astype(BF16), k_b)
        exp_g = jnp.exp(g_col)
        g_last = g_col[cc - 1:cc]
        att_scr[hh, rows] = (qk * decay).astype(BF16)
        qd_scr[hh, rows] = (q * exp_g).astype(BF16)
        kdt_scr[hh, :, rows] = (k * jnp.exp(g_last - g_col)).T.astype(BF16)
        dl_scr[hh, c] = jnp.broadcast_to(jnp.exp(g_last), (8, LANES))
        rhs = jnp.concatenate([v_h[rows] * beta_col, k * (beta_col * exp_g)], axis=1)
        pre_state.append((jnp.where(strict, beta_col * kk * decay, 0.0), rhs))

    low = [m for m, _ in pre_state]
    powers = [jnp.where(base_mask, -m, 0.0) for m in low]
    n_accs = list(powers)
    for _ in range(GDN_BASE_STEPS):
        powers = [_dot(p.astype(BF16), p.astype(BF16)) for p in powers]
        n_accs = [n + p + _dot(n.astype(BF16), p.astype(BF16)) for n, p in zip(n_accs, powers)]
    invs = [eye_f + n for n in n_accs]
    for off_mask in merge_masks:
        inv_b = [t.astype(BF16) for t in invs]
        corr = [_dot(jnp.where(off_mask, m, 0.0).astype(BF16), t) for m, t in zip(low, inv_b)]
        invs = [t - _dot(t_b, e.astype(BF16)) for t, t_b, e in zip(invs, inv_b, corr)]
    for (hh, c), inv, (_, rhs) in zip(problems, invs, pre_state):
        rows = slice(c * cc, (c + 1) * cc)
        sol = rhs + _dot((inv - eye_f).astype(BF16), rhs.astype(BF16))
        u_scr[hh, rows] = sol[:, :GDN_DV]
        w_scr[hh, rows] = sol[:, GDN_DV:].astype(BF16)

    state = [s_scr[hh] for hh in range(group)]
    heads = range(group)
    for c in range(n_chunks):
        rows = slice(c * cc, (c + 1) * cc)
        s_b = [state[hh].astype(BF16) for hh in heads]
        v_b = [(u_scr[hh, rows] - _dot(w_scr[hh, rows], s_b[hh])).astype(BF16) for hh in heads]
        state = [state[hh] * dl_scr[hh, c, 0:1, :] + _dot(kdt_scr[hh, :, rows], v_b[hh]) for hh in heads]
        for hh in heads:
            oc_scr[rows, hh * LANES:(hh + 1) * LANES] = _dot(qd_scr[hh, rows], s_b[hh]) + _dot(att_scr[hh, rows], v_b[hh])
    for hh in heads:
        s_scr[hh] = state[hh]

    for hh in range(group):
        lanes = slice(hh * LANES, (hh + 1) * LANES)
        og = og_ref[0, :, lanes].astype(F32)
        o_ref[0, :, lanes] = (_rms(oc_scr[:, lanes], gain_ref[...]) * (og * _sigmoid(og))).astype(o_ref.dtype)


def gated_deltanet(proj, gates, conv_w, a_log, dt_bias, gain):
    bsz, seq, _ = proj.shape
    n_heads = GDN_HEADS
    blk = GDN_BLOCK

    grp = GDN_GROUP
    n_groups = n_heads // grp
    width = grp * LANES

    def col(part):
        return pl.BlockSpec((1, blk, width), lambda b, h, t: (b, t, part * n_groups + h))

    def conv_spec(part):
        return pl.BlockSpec((GDN_CONV, width), lambda b, h, t: (0, part * n_groups + h))

    small = pl.BlockSpec((n_heads, 1), lambda b, h, t: (0, 0))
    return pl.pallas_call(
        _gdn_body,
        grid=(bsz, n_groups, seq // blk),
        in_specs=[col(0), col(1), col(2), col(3),
                  pl.BlockSpec((1, 2 * n_heads, blk), lambda b, h, t: (b, 0, t)),
                  conv_spec(0), conv_spec(1), conv_spec(2), small, small,
                  pl.BlockSpec((1, GDN_DV), lambda b, h, t: (0, 0))],
        out_specs=pl.BlockSpec((1, blk, width), lambda b, h, t: (b, t, h)),
        out_shape=jax.ShapeDtypeStruct((bsz, seq, n_heads * GDN_DV), BF16),
        scratch_shapes=[pltpu.VMEM((grp, GDN_DK, GDN_DV), F32),
                        pltpu.VMEM((3, CONV_PAD, width), F32),
                        pltpu.VMEM((grp, blk, GDN_DV), F32),
                        pltpu.VMEM((grp, blk, GDN_DK), BF16),
                        pltpu.VMEM((grp, blk, GDN_CC), BF16),
                        pltpu.VMEM((grp, blk, GDN_DK), BF16),
                        pltpu.VMEM((grp, GDN_DK, blk), BF16),
                        pltpu.VMEM((grp, blk // GDN_CC, 8, LANES), F32),
                        pltpu.VMEM((blk, width), F32)],
        compiler_params=_params("parallel", "parallel", "arbitrary"),
        name="gdn",
    )(proj, proj, proj, proj, gates, conv_w, conv_w, conv_w, a_log, dt_bias, gain)


def _xattn_body(*refs, n_acts, n_heads):
    x_ref = refs[0]
    a_refs = refs[1:1 + n_acts]
    w_refs = refs[1 + n_acts:1 + 2 * n_acts]
    g_ref, wq_ref, k_ref, v_ref, wo_ref, o_ref = refs[1 + 2 * n_acts:]
    x = x_ref[...]
    for a_ref, w_ref in zip(a_refs, w_refs):
        x = x + _dot(a_ref[...], w_ref[...])
    d = x.shape[1]
    dh = d // n_heads
    h = _rms(x, g_ref[...]).astype(BF16)
    q = _dot(h, wq_ref[...]).astype(BF16)
    outs = []
    for hd in range(n_heads):
        lanes = slice(hd * dh, (hd + 1) * dh)
        s = _dot_nt(q[:, lanes], k_ref[0, :, lanes]) * (dh ** -0.5)
        e = jnp.exp(s - jnp.max(s, axis=-1, keepdims=True))
        p = e / jnp.sum(e, axis=-1, keepdims=True)
        outs.append(_dot(p.astype(BF16), v_ref[0, :, lanes]).astype(BF16))
    o_ref[...] = x + _dot(jnp.concatenate(outs, axis=1), wo_ref[...])


def mix_out_xattn(x, acts, w_outs, gain, wq, k, v, wo, seq_len, tm=512):
    n, d = x.shape
    mem_len = k.shape[1]
    tiles_per_seq = seq_len // tm
    in_specs = [pl.BlockSpec((tm, d), lambda i: (i, 0))]
    in_specs += [pl.BlockSpec((tm, a.shape[1]), lambda i: (i, 0)) for a in acts]
    in_specs += [pl.BlockSpec(w.shape, lambda i: (0, 0)) for w in w_outs]
    in_specs += [pl.BlockSpec((1, d), lambda i: (0, 0)),
                 pl.BlockSpec((d, d), lambda i: (0, 0)),
                 pl.BlockSpec((1, mem_len, d), lambda i: (i // tiles_per_seq, 0, 0)),
                 pl.BlockSpec((1, mem_len, d), lambda i: (i // tiles_per_seq, 0, 0)),
                 pl.BlockSpec((d, d), lambda i: (0, 0))]
    return pl.pallas_call(
        functools.partial(_xattn_body, n_acts=len(acts), n_heads=XA_HEADS),
        grid=(n // tm,),
        in_specs=in_specs,
        out_specs=pl.BlockSpec((tm, d), lambda i: (i, 0)),
        out_shape=jax.ShapeDtypeStruct((n, d), F32),
        compiler_params=_params("parallel"),
        name="xattn",
    )(x, *acts, *w_outs, gain.reshape(1, d), wq, k, v, wo)


def _mlp_body(x_ref, g_ref, w1_ref, w2_ref, fg_ref, o_ref, h_scr, acc_scr, *, final_norm):
    f = pl.program_id(1)

    @pl.when(f == 0)
    def _():
        h_scr[...] = _rms(x_ref[...], g_ref[...]).astype(BF16)
        acc_scr[...] = jnp.zeros_like(acc_scr)

    a = jnp.maximum(_dot(h_scr[...], w1_ref[...]), 0.0)
    acc_scr[...] += _dot((a * a).astype(BF16), w2_ref[...])

    @pl.when(f == pl.num_programs(1) - 1)
    def _():
        y = x_ref[...] + acc_scr[...]
        o_ref[...] = _rms(y, fg_ref[...]) if final_norm else y


def mlp(x, gain, w1, w2, final_gain, final_norm, tm=1024, tf=1024):
    n, d = x.shape
    d_ff = w1.shape[1]
    return pl.pallas_call(
        functools.partial(_mlp_body, final_norm=final_norm),
        grid=(n // tm, d_ff // tf),
        in_specs=[pl.BlockSpec((tm, d), lambda i, f: (i, 0)),
                  pl.BlockSpec((1, d), lambda i, f: (0, 0)),
                  pl.BlockSpec((d, tf), lambda i, f: (0, f)),
                  pl.BlockSpec((tf, d), lambda i, f: (f, 0)),
                  pl.BlockSpec((1, d), lambda i, f: (0, 0))],
        out_specs=pl.BlockSpec((tm, d), lambda i, f: (i, 0)),
        out_shape=jax.ShapeDtypeStruct((n, d), F32),
        scratch_shapes=[pltpu.VMEM((tm, d), BF16), pltpu.VMEM((tm, d), F32)],
        compiler_params=_params("parallel", "arbitrary"),
        name="mlp",
    )(x, gain.reshape(1, d), w1, w2, final_gain.reshape(1, d))


def _pad_heads(w, n_heads, dh):
    d = w.shape[0]
    w = w.reshape(d, n_heads, dh)
    return jnp.pad(w, ((0, 0), (0, 0), (0, LANES - dh))).reshape(d, n_heads * LANES)


def _even_layer(x2, seq_len, gain, w_in, b_i, b_f, head_gain, w_out):
    ml_qk = MLSTM_HEADS * MLSTM_DQK
    ml_v = MLSTM_HEADS * MLSTM_DV
    sb_w = SB_HEADS * SB_DH
    sizes = (ml_qk, ml_qk, ml_v, ml_v, MLSTM_HEADS, MLSTM_HEADS, sb_w, sb_w, sb_w)
    offs = [0]
    for s in sizes:
        offs.append(offs[-1] + s)
    part = [w_in[:, offs[i]:offs[i + 1]] for i in range(len(sizes))]
    mq, mk, mv, mo, mi, mf, sq, sk, sv = part
    w_main = jnp.concatenate([
        _pad_heads(mq, MLSTM_HEADS, MLSTM_DQK),
        _pad_heads(mk * (MLSTM_DQK ** -0.5), MLSTM_HEADS, MLSTM_DQK),
        mv, sq * SB_Q_SCALE, sk, sv], axis=1).astype(BF16)
    w_gate_t = jnp.concatenate([mi, mf], axis=1).T.astype(BF16)
    proj, og, gates = norm_proj(x2, gain, [w_main, mo.astype(BF16)], [BF16, F32], gate_w_t=w_gate_t, seq_len=seq_len)
    bsz = x2.shape[0] // seq_len
    proj3 = proj.reshape(bsz, seq_len, proj.shape[1])
    bias = jnp.concatenate([b_i, b_f]).reshape(2 * MLSTM_HEADS, 1)
    hm = mlstm(proj3, og.reshape(bsz, seq_len, ml_v), gates, bias, head_gain.reshape(1, ml_v), 0, 1, 2)
    hs = stick_breaking(proj3, 3 * SB_HEADS, 4 * SB_HEADS, 5 * SB_HEADS)
    w_out = w_out.astype(BF16)
    return [hm.reshape(-1, ml_v), hs.reshape(-1, sb_w)], [w_out[:ml_v], w_out[ml_v:]]


def _odd_layer(x2, seq_len, gain, w_in, conv_w, a_log, dt_bias, head_gain, w_out):
    conv_ch = 3 * GDN_HEADS * GDN_DK
    vw = GDN_HEADS * GDN_DV
    w_main = w_in[:, :conv_ch + vw].astype(BF16)
    w_gate_t = w_in[:, conv_ch + vw:].T.astype(BF16)
    proj, gates = norm_proj(x2, gain, [w_main], [BF16], gate_w_t=w_gate_t, seq_len=seq_len)
    bsz = x2.shape[0] // seq_len
    o = gated_deltanet(proj.reshape(bsz, seq_len, conv_ch + vw), gates, conv_w,
                       a_log.reshape(GDN_HEADS, 1), dt_bias.reshape(GDN_HEADS, 1), head_gain.reshape(1, GDN_DV))
    return [o.reshape(-1, vw)], [w_out.astype(BF16)]


def kernel(x, mem, mix_norm, ab_w_in, ab_b_i, ab_b_f, ab_head_gain, ab_w_out, c_w_in, c_conv_w, c_a_log, c_dt_bias, c_head_gain, c_w_out, xa_norm, mem_norm, xa_wq, xa_wk, xa_wv, xa_wo, mlp_norm, mlp_w1, mlp_w2, final_norm):
    bsz, seq_len, d = x.shape
    mem_len = mem.shape[1]
    depth = mix_norm.shape[0]
    x2 = x.reshape(bsz * seq_len, d)
    mem2 = mem.reshape(bsz * mem_len, d)
    for l in range(depth):
        j = l // 2
        if l % 2 == 0:
            acts, w_outs = _even_layer(x2, seq_len, mix_norm[l], ab_w_in[j], ab_b_i[j], ab_b_f[j], ab_head_gain[j], ab_w_out[j])
        else:
            acts, w_outs = _odd_layer(x2, seq_len, mix_norm[l], c_w_in[j], c_conv_w[j], c_a_log[j], c_dt_bias[j], c_head_gain[j], c_w_out[j])
        mem_k, mem_v = norm_proj(mem2, mem_norm, [xa_wk[l].astype(BF16), xa_wv[l].astype(BF16)], [BF16, BF16], tm=mem_len)
        x2 = mix_out_xattn(x2, acts, w_outs, xa_norm[l], xa_wq[l].astype(BF16), mem_k.reshape(bsz, mem_len, d),
                           mem_v.reshape(bsz, mem_len, d), xa_wo[l].astype(BF16), seq_len)
        x2 = mlp(x2, mlp_norm[l], mlp_w1[l].astype(BF16), mlp_w2[l].astype(BF16), final_norm, l == depth - 1)
    return x2.reshape(bsz, seq_len, d)
```

```python
import functools

import jax
import jax.numpy as jnp
from jax import lax
from jax.experimental import pallas as pl
from jax.experimental.pallas import tpu as pltpu

F32 = jnp.float32
BF16 = jnp.bfloat16

EPS = 1e-6
GATE_SOFTCAP = 15.0

MLSTM_HEADS = 4
MLSTM_DQK = 64
MLSTM_DV = 128
SB_HEADS = 4
SB_DH = 128
GDN_HEADS = 8
GDN_DK = 128
GDN_DV = 128
GDN_CONV = 4
XA_HEADS = 4

LANES = 128
VMEM_LIMIT_BYTES = 48 * 1024 * 1024

NT_DIMS = (((1,), (1,)), ((), ()))


def _params(*semantics):
    return pltpu.CompilerParams(dimension_semantics=semantics, vmem_limit_bytes=VMEM_LIMIT_BYTES)


def _dot(a, b):
    return jnp.dot(a, b, preferred_element_type=F32)


def _dot_nt(a, b):
    return lax.dot_general(a, b, NT_DIMS, preferred_element_type=F32)


def _rms(x, g):
    return x * lax.rsqrt(jnp.mean(x * x, axis=-1, keepdims=True) + EPS) * g


def _sigmoid(x):
    return 1.0 / (1.0 + jnp.exp(-x))


def _softplus(x):
    return jnp.maximum(x, 0.0) + jnp.log(1.0 + jnp.exp(-jnp.abs(x)))


def _chunk_cumsum(x, chunk):
    pos = lax.broadcasted_iota(jnp.int32, x.shape, 1) % chunk
    shift = 1
    while shift < chunk:
        x = x + jnp.where(pos >= shift, pltpu.roll(x, shift, axis=1), 0.0)
        shift *= 2
    return x


def _col_from_row(row, eye):
    n = row.shape[1]
    return jnp.sum(jnp.where(eye, jnp.broadcast_to(row, (n, n)), 0.0), axis=1, keepdims=True)


PROJ_COL_CHUNK = 1024


def _norm_proj_body(*refs, n_out, has_gate):
    x_ref, g_ref = refs[0], refs[1]
    w_refs = refs[2:2 + n_out]
    pos = 2 + n_out
    wg_ref = refs[pos] if has_gate else None
    pos += int(has_gate)
    out_refs = refs[pos:pos + n_out]
    pos += n_out
    h = _rms(x_ref[...], g_ref[...]).astype(BF16)
    for w_ref, o_ref in zip(w_refs, out_refs):
        width = w_ref.shape[1]
        for c0 in range(0, width, PROJ_COL_CHUNK):
            c1 = min(c0 + PROJ_COL_CHUNK, width)
            o_ref[:, c0:c1] = _dot(h, w_ref[:, c0:c1]).astype(o_ref.dtype)
    if has_gate:
        refs[pos][0] = _dot_nt(wg_ref[...], h)


def norm_proj(x, gain, weights, out_dtypes, gate_w_t=None, seq_len=None, tm=512):
    n, d = x.shape
    has_gate = gate_w_t is not None
    in_specs = [pl.BlockSpec((tm, d), lambda i: (i, 0)), pl.BlockSpec((1, d), lambda i: (0, 0))]
    in_specs += [pl.BlockSpec(w.shape, lambda i: (0, 0)) for w in weights]
    args = [x, gain.reshape(1, d)] + list(weights)
    out_shape = [jax.ShapeDtypeStruct((n, w.shape[1]), dt) for w, dt in zip(weights, out_dtypes)]
    out_specs = [pl.BlockSpec((tm, w.shape[1]), lambda i: (i, 0)) for w in weights]
    if has_gate:
        n_gates = gate_w_t.shape[0]
        tiles_per_seq = seq_len // tm
        in_specs.append(pl.BlockSpec(gate_w_t.shape, lambda i: (0, 0)))
        args.append(gate_w_t)
        out_shape.append(jax.ShapeDtypeStruct((n // seq_len, n_gates, seq_len), F32))
        out_specs.append(pl.BlockSpec((1, n_gates, tm), lambda i: (i // tiles_per_seq, 0, i % tiles_per_seq)))
    return pl.pallas_call(
        functools.partial(_norm_proj_body, n_out=len(weights), has_gate=has_gate),
        grid=(n // tm,),
        in_specs=in_specs,
        out_specs=out_specs,
        out_shape=out_shape,
        compiler_params=_params("parallel"),
        name="norm_proj",
    )(*args)


CONV_PAD = 8


def _silu(y):
    half = 0.5 * y
    return half + half * jnp.tanh(half)


def _gdn_in_body(x_ref, g_ref, w_ref, cw_ref, wg_ref, o_ref, gt_ref, hist_scr, *, tiles_per_seq):
    tm = x_ref.shape[0]
    group_w = GDN_HEADS * GDN_DK

    @pl.when(pl.program_id(0) % tiles_per_seq == 0)
    def _():
        hist_scr[...] = jnp.zeros_like(hist_scr)

    h = _rms(x_ref[...], g_ref[...]).astype(BF16)
    sub_idx = lax.broadcasted_iota(jnp.int32, (tm // CONV_PAD, CONV_PAD, group_w), 1)
    n_parts = w_ref.shape[1] // group_w
    proj = [_dot(h, w_ref[:, part * group_w:(part + 1) * group_w]) for part in range(n_parts)]
    gt_ref[0] = _dot_nt(wg_ref[...], h)
    for part in range(3):
        cols = slice(part * group_w, (part + 1) * group_w)
        p = proj[part]
        tiles = p.reshape(tm // CONV_PAD, CONV_PAD, group_w)
        prev = hist_scr[part].reshape(1, CONV_PAD, group_w)
        y = tiles * cw_ref[GDN_CONV - 1:GDN_CONV, cols]
        rot, rot_prev = tiles, prev
        for lag in range(1, GDN_CONV):
            rot = pltpu.roll(rot, 1, axis=1)
            rot_prev = pltpu.roll(rot_prev, 1, axis=1)
            rot_before = jnp.concatenate([rot_prev, rot[:-1]], axis=0)
            y = y + jnp.where(sub_idx >= lag, rot, rot_before) * cw_ref[GDN_CONV - 1 - lag:GDN_CONV - lag, cols]
        hist_scr[part] = p[tm - CONV_PAD:, :]
        o_ref[:, cols] = _silu(y.reshape(tm, group_w)).astype(o_ref.dtype)
    cols = slice(3 * group_w, 4 * group_w)
    o_ref[:, cols] = _silu(proj[3]).astype(o_ref.dtype)


def gdn_in_proj(x, gain, w_main, conv_w, gate_w_t, seq_len, tm=512):
    n, d = x.shape
    width = w_main.shape[1]
    n_gates = gate_w_t.shape[0]
    tiles_per_seq = seq_len // tm
    return pl.pallas_call(
        functools.partial(_gdn_in_body, tiles_per_seq=tiles_per_seq),
        grid=(n // tm,),
        in_specs=[pl.BlockSpec((tm, d), lambda i: (i, 0)),
                  pl.BlockSpec((1, d), lambda i: (0, 0)),
                  pl.BlockSpec(w_main.shape, lambda i: (0, 0)),
                  pl.BlockSpec(conv_w.shape, lambda i: (0, 0)),
                  pl.BlockSpec(gate_w_t.shape, lambda i: (0, 0))],
        out_specs=[pl.BlockSpec((tm, width), lambda i: (i, 0)),
                   pl.BlockSpec((1, n_gates, tm), lambda i: (i // tiles_per_seq, 0, i % tiles_per_seq))],
        out_shape=[jax.ShapeDtypeStruct((n, width), BF16),
                   jax.ShapeDtypeStruct((n // seq_len, n_gates, seq_len), F32)],
        scratch_shapes=[pltpu.VMEM((3, CONV_PAD, GDN_HEADS * GDN_DK), F32)],
        compiler_params=_params("arbitrary"),
        name="gdn_in_proj",
    )(x, gain.reshape(1, d), w_main, conv_w, gate_w_t)


MLSTM_BLOCK = 512
MLSTM_CC = 128


def _mlstm_body(q_ref, k_ref, v_ref, og_ref, gate_ref, bias_ref, gain_ref, o_ref, c_scr, m_scr, ab_scr):
    n_heads = MLSTM_HEADS
    cc = MLSTM_CC
    block = q_ref.shape[1]

    @pl.when(pl.program_id(1) == 0)
    def _():
        c_scr[...] = jnp.zeros_like(c_scr)
        m_scr[...] = jnp.zeros_like(m_scr)

    pre = gate_ref[0] + bias_ref[...]
    log_i = GATE_SOFTCAP * jnp.tanh(pre[:n_heads] / GATE_SOFTCAP)
    log_f = -_softplus(-pre[n_heads:])
    b = _chunk_cumsum(log_f, cc)
    ab_scr[:n_heads] = log_i - b
    ab_scr[n_heads:] = b

    r_idx = lax.broadcasted_iota(jnp.int32, (cc, cc), 0)
    c_idx = lax.broadcasted_iota(jnp.int32, (cc, cc), 1)
    eye = r_idx == c_idx
    causal = c_idx <= r_idx
    ones_col = (lax.broadcasted_iota(jnp.int32, (cc, LANES), 1) == 0).astype(BF16)

    heads = range(n_heads)
    lanes = [slice(h * LANES, (h + 1) * LANES) for h in heads]
    m_prev = [m_scr[h:h + 1, 0:1] for h in heads]
    c_aug = [c_scr[h] for h in heads]
    for ci in range(block // cc):
        rows = slice(ci * cc, (ci + 1) * cc)
        q = [q_ref[0, rows, lanes[h]] for h in heads]
        k_t = [k_ref[0, rows, lanes[h]].astype(F32).T for h in heads]
        v_aug = [jnp.concatenate([v_ref[0, rows, lanes[h]], ones_col], axis=1) for h in heads]
        a_row = [ab_scr[h:h + 1, rows] for h in heads]
        b_col = [_col_from_row(ab_scr[n_heads + h:n_heads + h + 1, rows], eye) for h in heads]
        d = [jnp.where(causal, jnp.broadcast_to(a_row[h], (cc, cc)), -jnp.inf) for h in heads]
        g_col = [jnp.maximum(jnp.max(d[h], axis=1, keepdims=True), m_prev[h]) for h in heads]
        qk = [_dot(q[h], k_t[h].astype(BF16)) for h in heads]
        inter = [_dot(q[h], c_aug[h].astype(BF16)) for h in heads]
        s = [(qk[h] * jnp.exp(d[h] - g_col[h])).astype(BF16) for h in heads]
        g_last = [g_col[h][cc - 1:cc] for h in heads]
        kv = [_dot((k_t[h] * jnp.exp(a_row[h] - g_last[h])).astype(BF16), v_aug[h]) for h in heads]
        numden = [_dot(s[h], v_aug[h]) + jnp.exp(m_prev[h] - g_col[h]) * inter[h] for h in heads]
        c_aug = [jnp.exp(m_prev[h] - g_last[h]) * c_aug[h] + kv[h] for h in heads]
        m_prev = [b_col[h][cc - 1:cc] + g_last[h] for h in heads]
        for h in heads:
            num = numden[h][:, :MLSTM_DV]
            den = numden[h][:, MLSTM_DV:MLSTM_DV + 1]
            hh = num / jnp.maximum(jnp.abs(den), jnp.exp(-(b_col[h] + g_col[h])))
            hn = _rms(hh, gain_ref[:, lanes[h]]) * _sigmoid(og_ref[0, rows, lanes[h]])
            o_ref[0, rows, lanes[h]] = hn.astype(o_ref.dtype)
    for h in heads:
        c_scr[h] = c_aug[h]
        m_scr[h:h + 1, :] = jnp.broadcast_to(m_prev[h], (1, LANES))


def mlstm(proj, og, gates, bias, gain, q_blk, k_blk, v_blk):
    bsz, seq, _ = proj.shape
    width = MLSTM_HEADS * LANES
    blk = MLSTM_BLOCK

    def col(cb):
        return pl.BlockSpec((1, blk, width), lambda b, t: (b, t, cb))

    return pl.pallas_call(
        _mlstm_body,
        grid=(bsz, seq // blk),
        in_specs=[col(q_blk), col(k_blk), col(v_blk),
                  pl.BlockSpec((1, blk, width), lambda b, t: (b, t, 0)),
                  pl.BlockSpec((1, 2 * MLSTM_HEADS, blk), lambda b, t: (b, 0, t)),
                  pl.BlockSpec((2 * MLSTM_HEADS, 1), lambda b, t: (0, 0)),
                  pl.BlockSpec((1, width), lambda b, t: (0, 0))],
        out_specs=pl.BlockSpec((1, blk, width), lambda b, t: (b, t, 0)),
        out_shape=jax.ShapeDtypeStruct((bsz, seq, width), BF16),
        scratch_shapes=[pltpu.VMEM((MLSTM_HEADS, LANES, 2 * LANES), F32),
                        pltpu.VMEM((8, LANES), F32),
                        pltpu.VMEM((2 * MLSTM_HEADS, blk), F32)],
        compiler_params=_params("parallel", "arbitrary"),
        name="mlstm",
    )(proj, proj, proj, og, gates, bias, gain)


SB_TQ = 1024
SB_KT = 256
SB_KB = 256
SIGN_BIT = 0x80000000
SB_DEAD_LOG2 = -150.0
SB_Q_SCALE = -(SB_DH ** -0.5) * 1.4426950408889634


def _sb_body(q_ref, k_ref, v_ref, o_ref, acc_scr, run_scr):
    tq = q_ref.shape[1]
    kt = SB_KT
    i = pl.program_id(2)
    acc_scr[...] = jnp.zeros_like(acc_scr)
    run_scr[...] = jnp.zeros_like(run_scr)

    suffix = (lax.broadcasted_iota(jnp.int32, (kt, kt), 0) > lax.broadcasted_iota(jnp.int32, (kt, kt), 1)).astype(BF16)

    def block(r0, k_start, n_tiles, diag_off):
        rows = tq - r0
        width = n_tiles * kt
        keys = pl.ds(pl.multiple_of(k_start, kt), width)
        zn = _dot_nt(q_ref[0, r0:, :], k_ref[0, keys, :])
        neg_abs = lax.bitcast_convert_type(lax.bitcast_convert_type(zn, jnp.uint32) | jnp.uint32(SIGN_BIT), F32)
        sp = jnp.log2(1.0 + jnp.exp2(neg_abs))
        log_1m = jnp.minimum(zn, 0.0) - sp
        log_beta = log_1m - zn
        mask = None
        if diag_off is not None:
            q_pos = lax.broadcasted_iota(jnp.int32, (rows, width), 0) + r0
            k_pos = lax.broadcasted_iota(jnp.int32, (rows, width), 1) + diag_off
            mask = k_pos < q_pos
            log_1m = jnp.where(mask, log_1m, 0.0)
        run = run_scr[r0:, :]
        weights = [None] * n_tiles
        for t in reversed(range(n_tiles)):
            cols = slice(t * kt, (t + 1) * kt)
            sums = _dot(log_1m[:, cols].astype(BF16), suffix)
            w = jnp.exp2(log_beta[:, cols] + sums + jnp.concatenate([run] * (kt // LANES), axis=1))
            if mask is not None:
                w = jnp.where(mask[:, cols], w, 0.0)
            weights[t] = w.astype(BF16)
            run = run + jnp.sum(log_1m[:, cols], axis=-1, keepdims=True)
        acc_scr[r0:, :] += _dot(jnp.concatenate(weights, axis=1), v_ref[0, keys, :])
        run_scr[r0:, :] = run

    for t in reversed(range(tq // kt)):
        block(t * kt, i * tq + t * kt, 1, t * kt)

    n_blocks = i * (tq // SB_KB)

    def live(step):
        return jnp.logical_and(step < n_blocks, jnp.max(run_scr[...]) > SB_DEAD_LOG2)

    def body(step):
        block(0, (n_blocks - 1 - step) * SB_KB, SB_KB // kt, None)
        return step + 1

    lax.while_loop(live, body, 0)
    o_ref[0] = acc_scr[...].astype(o_ref.dtype)


def stick_breaking(proj, q_blk, k_blk, v_blk):
    bsz, seq, _ = proj.shape
    tq = SB_TQ
    return pl.pallas_call(
        _sb_body,
        grid=(bsz, SB_HEADS, seq // tq),
        in_specs=[pl.BlockSpec((1, tq, SB_DH), lambda b, h, i: (b, i, q_blk + h)),
                  pl.BlockSpec((1, seq, SB_DH), lambda b, h, i: (b, 0, k_blk + h)),
                  pl.BlockSpec((1, seq, SB_DH), lambda b, h, i: (b, 0, v_blk + h))],
        out_specs=pl.BlockSpec((1, tq, SB_DH), lambda b, h, i: (b, i, h)),
        out_shape=jax.ShapeDtypeStruct((bsz, seq, SB_HEADS * SB_DH), BF16),
        scratch_shapes=[pltpu.VMEM((tq, SB_DH), F32), pltpu.VMEM((tq, LANES), F32)],
        compiler_params=_params("parallel", "parallel", "arbitrary"),
        name="stickbreak",
    )(proj, proj, proj)


GDN_BLOCK = 512
GDN_CC = 128
GDN_GROUP = 8
GDN_BASE = 8
GDN_BASE_STEPS = 2


def _gdn_body(q_ref, k_ref, v_ref, og_ref, gate_ref, alog_ref, dtb_ref, gain_ref,
              o_ref, s_scr, u_scr, w_scr, att_scr, qd_scr, kdt_scr, dl_scr, oc_scr):
    block = q_ref.shape[1]
    cc = GDN_CC
    n_chunks = block // cc
    n_heads = GDN_HEADS
    group = GDN_GROUP

    @pl.when(pl.program_id(2) == 0)
    def _():
        s_scr[...] = jnp.zeros_like(s_scr)

    pre = gate_ref[0]
    beta_all = _sigmoid(pre[:n_heads])
    log_alpha = -jnp.exp(alog_ref[...]) * _softplus(pre[n_heads:] + dtb_ref[...])
    g_all = _chunk_cumsum(log_alpha, cc)

    r_idx = lax.broadcasted_iota(jnp.int32, (cc, cc), 0)
    c_idx = lax.broadcasted_iota(jnp.int32, (cc, cc), 1)
    eye = r_idx == c_idx
    incl = c_idx <= r_idx
    strict = c_idx < r_idx
    eye_f = eye.astype(F32)
    base_mask = (r_idx // GDN_BASE) == (c_idx // GDN_BASE)
    merge_masks = []
    size = GDN_BASE
    while size < cc:
        merge_masks.append(((r_idx // (2 * size)) == (c_idx // (2 * size))) & ((r_idx // size) != (c_idx // size)))
        size *= 2
    head_idx =lax.broadcasted_iota(jnp.int32, (n_heads, block), 0)

    problems = [(hh, c) for hh in range(group) for c in range(n_chunks)]
    head_gates = []
    for hh in range(group):
        is_head = head_idx == pl.program_id(1) * group + hh
        g_head = jnp.sum(jnp.where(is_head, g_all, 0.0), axis=0, keepdims=True)
        beta_head = jnp.sum(jnp.where(is_head, beta_all, 0.0), axis=0, keepdims=True)
        head_gates.append((g_head, beta_head))

    def l2n(x, scale):
        return x * (lax.rsqrt(jnp.sum(x * x, axis=-1, keepdims=True) + EPS) * scale)

    pre_state = []
    for hh, c in problems:
        rows = slice(c * cc, (c + 1) * cc)
        lanes = slice(hh * LANES, (hh + 1) * LANES)
        g_head, beta_head = head_gates[hh]
        q = l2n(q_ref[0, rows, lanes].astype(F32), GDN_DK ** -0.5)
        k = l2n(k_ref[0, rows, lanes].astype(F32), 1.0)
        q_b = q.astype(BF16)
        k_b = k.astype(BF16)
        v = v_ref[0, rows, lanes].astype(F32)
        g_row = g_head[:, rows]
        g_col = _col_from_row(g_row, eye)
        beta_col = _col_from_row(beta_head[:, rows], eye)
        decay = jnp.where(incl, jnp.exp(g_col - g_row), 0.0)
        kk = _dot_nt(k_b, k_b)
        qk = _dot_nt(q_b, k_b)
        exp_g = jnp.exp(g_col)
        g_last = g_col[cc - 1:cc]
        att_scr[hh, rows] = (qk * decay).astype(BF16)
        qd_scr[hh, rows] = (q * exp_g).astype(BF16)
        kdt_scr[hh, :, rows] = (k * jnp.exp(g_last - g_col)).T.astype(BF16)
        dl_scr[hh, c] = jnp.broadcast_to(jnp.exp(g_last), (8, LANES))
        rhs = jnp.concatenate([v * beta_col, k * (beta_col * exp_g)], axis=1)
        pre_state.append((jnp.where(strict, beta_col * kk * decay, 0.0), rhs))

    low = [m for m, _ in pre_state]
    powers = [jnp.where(base_mask, -m, 0.0) for m in low]
    n_accs = list(powers)
    for _ in range(GDN_BASE_STEPS):
        powers = [_dot(p.astype(BF16), p.astype(BF16)) for p in powers]
        n_accs = [n + p + _dot(n.astype(BF16), p.astype(BF16)) for n, p in zip(n_accs, powers)]
    invs = [eye_f + n for n in n_accs]
    for off_mask in merge_masks:
        inv_b = [t.astype(BF16) for t in invs]
        corr = [_dot(jnp.where(off_mask, m, 0.0).astype(BF16), t) for m, t in zip(low, inv_b)]
        invs = [t - _dot(t_b, e.astype(BF16)) for t, t_b, e in zip(invs, inv_b, corr)]
    for (hh, c), inv, (_, rhs) in zip(problems, invs, pre_state):
        rows = slice(c * cc, (c + 1) * cc)
        sol = rhs + _dot((inv - eye_f).astype(BF16), rhs.astype(BF16))
        u_scr[hh, rows] = sol[:, :GDN_DV]
        w_scr[hh, rows] = sol[:, GDN_DV:].astype(BF16)

    state = [s_scr[hh] for hh in range(group)]
    heads = range(group)
    for c in range(n_chunks):
        rows = slice(c * cc, (c + 1) * cc)
        s_b = [state[hh].astype(BF16) for hh in heads]
        v_b = [(u_scr[hh, rows] - _dot(w_scr[hh, rows], s_b[hh])).astype(BF16) for hh in heads]
        state = [state[hh] * dl_scr[hh, c, 0:1, :] + _dot(kdt_scr[hh, :, rows], v_b[hh]) for hh in heads]
        for hh in heads:
            oc_scr[rows, hh * LANES:(hh + 1) * LANES] = _dot(qd_scr[hh, rows], s_b[hh]) + _dot(att_scr[hh, rows], v_b[hh])
    for hh in heads:
        s_scr[hh] = state[hh]

    for hh in range(group):
        lanes = slice(hh * LANES, (hh + 1) * LANES)
        og = og_ref[0, :, lanes].astype(F32)
        o_ref[0, :, lanes] = (_rms(oc_scr[:, lanes], gain_ref[...]) * og).astype(o_ref.dtype)


def gated_deltanet(proj, gates, a_log, dt_bias, gain):
    bsz, seq, _ = proj.shape
    n_heads = GDN_HEADS
    blk = GDN_BLOCK

    grp = GDN_GROUP
    n_groups = n_heads // grp
    width = grp * LANES

    def col(part):
        return pl.BlockSpec((1, blk, width), lambda b, h, t: (b, t, part * n_groups + h))

    small = pl.BlockSpec((n_heads, 1), lambda b, h, t: (0, 0))
    return pl.pallas_call(
        _gdn_body,
        grid=(bsz, n_groups, seq // blk),
        in_specs=[col(0), col(1), col(2), col(3),
                  pl.BlockSpec((1, 2 * n_heads, blk), lambda b, h, t: (b, 0, t)),
                  small, small,
                  pl.BlockSpec((1, GDN_DV), lambda b, h, t: (0, 0))],
        out_specs=pl.BlockSpec((1, blk, width), lambda b, h, t: (b, t, h)),
        out_shape=jax.ShapeDtypeStruct((bsz, seq, n_heads * GDN_DV), BF16),
        scratch_shapes=[pltpu.VMEM((grp, GDN_DK, GDN_DV), F32),
                        pltpu.VMEM((grp, blk, GDN_DV), F32),
                        pltpu.VMEM((grp, blk, GDN_DK), BF16),
                        pltpu.VMEM((grp, blk, GDN_CC), BF16),
                        pltpu.VMEM((grp, blk, GDN_DK), BF16),
                        pltpu.VMEM((grp, GDN_DK, blk), BF16),
                        pltpu.VMEM((grp, blk // GDN_CC, 8, LANES), F32),
                        pltpu.VMEM((blk, width), F32)],
        compiler_params=_params("parallel", "parallel", "arbitrary"),
        name="gdn",
    )(proj, proj, proj, proj, gates, a_log, dt_bias, gain)


def _xattn_body(*refs, n_acts, n_heads):
    x_ref = refs[0]
    a_refs = refs[1:1 + n_acts]
    w_refs = refs[1 + n_acts:1 + 2 * n_acts]
    g_ref, wq_ref, k_ref, v_ref, wo_ref, o_ref = refs[1 + 2 * n_acts:]
    x = x_ref[...]
    for a_ref, w_ref in zip(a_refs, w_refs):
        x = x + _dot(a_ref[...], w_ref[...])
    d = x.shape[1]
    dh = d // n_heads
    h = _rms(x, g_ref[...]).astype(BF16)
    q = _dot(h, wq_ref[...]).astype(BF16)
    outs = []
    for hd in range(n_heads):
        lanes = slice(hd * dh, (hd + 1) * dh)
        s = _dot_nt(q[:, lanes], k_ref[0, :, lanes]) * (dh ** -0.5)
        e = jnp.exp(s - jnp.max(s, axis=-1, keepdims=True))
        p = e / jnp.sum(e, axis=-1, keepdims=True)
        outs.append(_dot(p.astype(BF16), v_ref[0, :, lanes]).astype(BF16))
    o_ref[...] = x + _dot(jnp.concatenate(outs, axis=1), wo_ref[...])


def mix_out_xattn(x, acts, w_outs, gain, wq, k, v, wo, seq_len, tm=512):
    n, d = x.shape
    mem_len = k.shape[1]
    tiles_per_seq = seq_len // tm
    in_specs = [pl.BlockSpec((tm, d), lambda i: (i, 0))]
    in_specs += [pl.BlockSpec((tm, a.shape[1]), lambda i: (i, 0)) for a in acts]
    in_specs += [pl.BlockSpec(w.shape, lambda i: (0, 0)) for w in w_outs]
    in_specs += [pl.BlockSpec((1, d), lambda i: (0, 0)),
                 pl.BlockSpec((d, d), lambda i: (0, 0)),
                 pl.BlockSpec((1, mem_len, d), lambda i: (i // tiles_per_seq, 0, 0)),
                 pl.BlockSpec((1, mem_len, d), lambda i: (i // tiles_per_seq, 0, 0)),
                 pl.BlockSpec((d, d), lambda i: (0, 0))]
    return pl.pallas_call(
        functools.partial(_xattn_body, n_acts=len(acts), n_heads=XA_HEADS),
        grid=(n // tm,),
        in_specs=in_specs,
        out_specs=pl.BlockSpec((tm, d), lambda i: (i, 0)),
        out_shape=jax.ShapeDtypeStruct((n, d), F32),
        compiler_params=_params("parallel"),
        name="xattn",
    )(x, *acts, *w_outs, gain.reshape(1, d), wq, k, v, wo)


def _mlp_body(x_ref, g_ref, w1_ref, w2_ref, fg_ref, o_ref, h_scr, acc_scr, *, final_norm):
    f = pl.program_id(1)

    @pl.when(f == 0)
    def _():
        h_scr[...] = _rms(x_ref[...], g_ref[...]).astype(BF16)
        acc_scr[...] = jnp.zeros_like(acc_scr)

    a = jnp.maximum(_dot(h_scr[...], w1_ref[...]), 0.0)
    acc_scr[...] += _dot((a * a).astype(BF16), w2_ref[...])

    @pl.when(f == pl.num_programs(1) - 1)
    def _():
        y = x_ref[...] + acc_scr[...]
        o_ref[...] = _rms(y, fg_ref[...]) if final_norm else y


def mlp(x, gain, w1, w2, final_gain, final_norm, tm=1024, tf=1024):
    n, d = x.shape
    d_ff = w1.shape[1]
    return pl.pallas_call(
        functools.partial(_mlp_body, final_norm=final_norm),
        grid=(n // tm, d_ff // tf),
        in_specs=[pl.BlockSpec((tm, d), lambda i, f: (i, 0)),
                  pl.BlockSpec((1, d), lambda i, f: (0, 0)),
                  pl.BlockSpec((d, tf), lambda i, f: (0, f)),
                  pl.BlockSpec((tf, d), lambda i, f: (f, 0)),
                  pl.BlockSpec((1, d), lambda i, f: (0, 0))],
        out_specs=pl.BlockSpec((tm, d), lambda i, f: (i, 0)),
        out_shape=jax.ShapeDtypeStruct((n, d), F32),
        scratch_shapes=[pltpu.VMEM((tm, d), BF16), pltpu.VMEM((tm, d), F32)],
        compiler_params=_params("parallel", "arbitrary"),
        name="mlp",
    )(x, gain.reshape(1, d), w1, w2, final_gain.reshape(1, d))


def _pad_heads(w, n_heads, dh):
    d = w.shape[0]
    w = w.reshape(d, n_heads, dh)
    return jnp.pad(w, ((0, 0), (0, 0), (0, LANES - dh))).reshape(d, n_heads * LANES)


def _even_layer(x2, seq_len, gain, w_in, b_i, b_f, head_gain, w_out):
    ml_qk = MLSTM_HEADS * MLSTM_DQK
    ml_v = MLSTM_HEADS * MLSTM_DV
    sb_w = SB_HEADS * SB_DH
    sizes = (ml_qk, ml_qk, ml_v, ml_v, MLSTM_HEADS, MLSTM_HEADS, sb_w, sb_w, sb_w)
    offs = [0]
    for s in sizes:
        offs.append(offs[-1] + s)
    part = [w_in[:, offs[i]:offs[i + 1]] for i in range(len(sizes))]
    mq, mk, mv, mo, mi, mf, sq, sk, sv = part
    w_main = jnp.concatenate([
        _pad_heads(mq, MLSTM_HEADS, MLSTM_DQK),
        _pad_heads(mk * (MLSTM_DQK ** -0.5), MLSTM_HEADS, MLSTM_DQK),
        mv, sq * SB_Q_SCALE, sk, sv], axis=1).astype(BF16)
    w_gate_t = jnp.concatenate([mi, mf], axis=1).T.astype(BF16)
    proj, og, gates = norm_proj(x2, gain, [w_main, mo.astype(BF16)], [BF16, F32], gate_w_t=w_gate_t, seq_len=seq_len)
    bsz = x2.shape[0] // seq_len
    proj3 = proj.reshape(bsz, seq_len, proj.shape[1])
    bias = jnp.concatenate([b_i, b_f]).reshape(2 * MLSTM_HEADS, 1)
    hm = mlstm(proj3, og.reshape(bsz, seq_len, ml_v), gates, bias, head_gain.reshape(1, ml_v), 0, 1, 2)
    hs = stick_breaking(proj3, 3 * SB_HEADS, 4 * SB_HEADS, 5 * SB_HEADS)
    w_out = w_out.astype(BF16)
    return [hm.reshape(-1, ml_v), hs.reshape(-1, sb_w)], [w_out[:ml_v], w_out[ml_v:]]


def _odd_layer(x2, seq_len, gain, w_in, conv_w, a_log, dt_bias, head_gain, w_out):
    conv_ch = 3 * GDN_HEADS * GDN_DK
    vw = GDN_HEADS * GDN_DV
    w_main = w_in[:, :conv_ch + vw].astype(BF16)
    w_gate_t = w_in[:, conv_ch + vw:].T.astype(BF16)
    proj, gates = gdn_in_proj(x2, gain, w_main, conv_w, w_gate_t, seq_len)
    bsz = x2.shape[0] // seq_len
    o = gated_deltanet(proj.reshape(bsz, seq_len, conv_ch + vw), gates,
                       a_log.reshape(GDN_HEADS, 1), dt_bias.reshape(GDN_HEADS, 1), head_gain.reshape(1, GDN_DV))
    return [o.reshape(-1, vw)], [w_out.astype(BF16)]


def kernel(x, mem, mix_norm, ab_w_in, ab_b_i, ab_b_f, ab_head_gain, ab_w_out, c_w_in, c_conv_w, c_a_log, c_dt_bias, c_head_gain, c_w_out, xa_norm, mem_norm, xa_wq, xa_wk, xa_wv, xa_wo, mlp_norm, mlp_w1, mlp_w2, final_norm):
    bsz, seq_len, d = x.shape
    mem_len = mem.shape[1]
    depth = mix_norm.shape[0]
    x2 = x.reshape(bsz * seq_len, d)
    mem2 = mem.reshape(bsz * mem_len, d)
    for l in range(depth):
        j = l // 2
        if l % 2 == 0:
            acts, w_outs = _even_layer(x2, seq_len, mix_norm[l], ab_w_in[j], ab_b_i[j], ab_b_f[j], ab_head_gain[j], ab_w_out[j])
        else:
            acts, w_outs = _odd_layer(x2, seq_len, mix_norm[l], c_w_in[j], c_conv_w[j], c_a_log[j], c_dt_bias[j], c_head_gain[j], c_w_out[j])
        mem_k, mem_v = norm_proj(mem2, mem_norm, [xa_wk[l].astype(BF16), xa_wv[l].astype(BF16)], [BF16, BF16], tm=mem_len)
        x2 = mix_out_xattn(x2, acts, w_outs, xa_norm[l], xa_wq[l].astype(BF16), mem_k.reshape(bsz, mem_len, d),
                           mem_v.reshape(bsz, mem_len, d), xa_wo[l].astype(BF16), seq_len)
        x2 = mlp(x2, mlp_norm[l], mlp_w1[l].astype(BF16), mlp_w2[l].astype(BF16), final_norm, l == depth - 1)
    return x2.reshape(bsz, seq_len, d)
```

```python
import functools

import jax
import jax.numpy as jnp
from jax import lax
from jax.experimental import pallas as pl
from jax.experimental.pallas import tpu as pltpu

F32 = jnp.float32
BF16 = jnp.bfloat16

EPS = 1e-6
GATE_SOFTCAP = 15.0

MLSTM_HEADS = 4
MLSTM_DQK = 64
MLSTM_DV = 128
SB_HEADS = 4
SB_DH = 128
GDN_HEADS = 8
GDN_DK = 128
GDN_DV = 128
GDN_CONV = 4
XA_HEADS = 4

LANES = 128
VMEM_LIMIT_BYTES = 48 * 1024 * 1024

NT_DIMS = (((1,), (1,)), ((), ()))


def _params(*semantics):
    return pltpu.CompilerParams(dimension_semantics=semantics, vmem_limit_bytes=VMEM_LIMIT_BYTES)


def _dot(a, b):
    return jnp.dot(a, b, preferred_element_type=F32)


def _dot_nt(a, b):
    return lax.dot_general(a, b, NT_DIMS, preferred_element_type=F32)


def _rms(x, g):
    return x * lax.rsqrt(jnp.mean(x * x, axis=-1, keepdims=True) + EPS) * g


def _sigmoid(x):
    return 1.0 / (1.0 + jnp.exp(-x))


def _softplus(x):
    return jnp.maximum(x, 0.0) + jnp.log(1.0 + jnp.exp(-jnp.abs(x)))


def _chunk_cumsum(x, chunk):
    pos = lax.broadcasted_iota(jnp.int32, x.shape, 1) % chunk
    shift = 1
    while shift < chunk:
        x = x + jnp.where(pos >= shift, pltpu.roll(x, shift, axis=1), 0.0)
        shift *= 2
    return x


def _col_from_row(row, eye):
    n = row.shape[1]
    return jnp.sum(jnp.where(eye, jnp.broadcast_to(row, (n, n)), 0.0), axis=1, keepdims=True)


PROJ_COL_CHUNK = 1024


def _norm_proj_body(*refs, n_out, has_gate):
    x_ref, g_ref = refs[0], refs[1]
    w_refs = refs[2:2 + n_out]
    pos = 2 + n_out
    wg_ref = refs[pos] if has_gate else None
    pos += int(has_gate)
    out_refs = refs[pos:pos + n_out]
    pos += n_out
    h = _rms(x_ref[...], g_ref[...]).astype(BF16)
    for w_ref, o_ref in zip(w_refs, out_refs):
        width = w_ref.shape[1]
        for c0 in range(0, width, PROJ_COL_CHUNK):
            c1 = min(c0 + PROJ_COL_CHUNK, width)
            o_ref[:, c0:c1] = _dot(h, w_ref[:, c0:c1]).astype(o_ref.dtype)
    if has_gate:
        refs[pos][0] = _dot_nt(wg_ref[...], h)


def norm_proj(x, gain, weights, out_dtypes, gate_w_t=None, seq_len=None, tm=512):
    n, d = x.shape
    has_gate = gate_w_t is not None
    in_specs = [pl.BlockSpec((tm, d), lambda i: (i, 0)), pl.BlockSpec((1, d), lambda i: (0, 0))]
    in_specs += [pl.BlockSpec(w.shape, lambda i: (0, 0)) for w in weights]
    args = [x, gain.reshape(1, d)] + list(weights)
    out_shape = [jax.ShapeDtypeStruct((n, w.shape[1]), dt) for w, dt in zip(weights, out_dtypes)]
    out_specs = [pl.BlockSpec((tm, w.shape[1]), lambda i: (i, 0)) for w in weights]
    if has_gate:
        n_gates = gate_w_t.shape[0]
        tiles_per_seq = seq_len // tm
        in_specs.append(pl.BlockSpec(gate_w_t.shape, lambda i: (0, 0)))
        args.append(gate_w_t)
        out_shape.append(jax.ShapeDtypeStruct((n // seq_len, n_gates, seq_len), F32))
        out_specs.append(pl.BlockSpec((1, n_gates, tm), lambda i: (i // tiles_per_seq, 0, i % tiles_per_seq)))
    return pl.pallas_call(
        functools.partial(_norm_proj_body, n_out=len(weights), has_gate=has_gate),
        grid=(n // tm,),
        in_specs=in_specs,
        out_specs=out_specs,
        out_shape=out_shape,
        compiler_params=_params("parallel"),
        name="norm_proj",
    )(*args)


CONV_PAD = 8


def _silu(y):
    half = 0.5 * y
    return half + half * jnp.tanh(half)


def _gdn_in_body(x_ref, g_ref, w_ref, cw_ref, wg_ref, o_ref, gt_ref, hist_scr, *, tiles_per_seq):
    tm = x_ref.shape[0]
    group_w = GDN_HEADS * GDN_DK

    @pl.when(pl.program_id(0) % tiles_per_seq == 0)
    def _():
        hist_scr[...] = jnp.zeros_like(hist_scr)

    h = _rms(x_ref[...], g_ref[...]).astype(BF16)
    sub_idx = lax.broadcasted_iota(jnp.int32, (tm // CONV_PAD, CONV_PAD, group_w), 1)
    n_parts = w_ref.shape[1] // group_w
    proj = [_dot(h, w_ref[:, part * group_w:(part + 1) * group_w]) for part in range(n_parts)]
    gt_ref[0] = _dot_nt(wg_ref[...], h)
    for part in range(3):
        cols = slice(part * group_w, (part + 1) * group_w)
        p = proj[part]
        tiles = p.reshape(tm // CONV_PAD, CONV_PAD, group_w)
        prev = hist_scr[part].reshape(1, CONV_PAD, group_w)
        y = tiles * cw_ref[GDN_CONV - 1:GDN_CONV, cols]
        rot, rot_prev = tiles, prev
        for lag in range(1, GDN_CONV):
            rot = pltpu.roll(rot, 1, axis=1)
            rot_prev = pltpu.roll(rot_prev, 1, axis=1)
            rot_before = jnp.concatenate([rot_prev, rot[:-1]], axis=0)
            y = y + jnp.where(sub_idx >= lag, rot, rot_before) * cw_ref[GDN_CONV - 1 - lag:GDN_CONV - lag, cols]
        hist_scr[part] = p[tm - CONV_PAD:, :]
        o_ref[:, cols] = _silu(y.reshape(tm, group_w)).astype(o_ref.dtype)
    cols = slice(3 * group_w, 4 * group_w)
    o_ref[:, cols] = _silu(proj[3]).astype(o_ref.dtype)


def gdn_in_proj(x, gain, w_main, conv_w, gate_w_t, seq_len, tm=256):
    n, d = x.shape
    width = w_main.shape[1]
    n_gates = gate_w_t.shape[0]
    tiles_per_seq = seq_len // tm
    return pl.pallas_call(
        functools.partial(_gdn_in_body, tiles_per_seq=tiles_per_seq),
        grid=(n // tm,),
        in_specs=[pl.BlockSpec((tm, d), lambda i: (i, 0)),
                  pl.BlockSpec((1, d), lambda i: (0, 0)),
                  pl.BlockSpec(w_main.shape, lambda i: (0, 0)),
                  pl.BlockSpec(conv_w.shape, lambda i: (0, 0)),
                  pl.BlockSpec(gate_w_t.shape, lambda i: (0, 0))],
        out_specs=[pl.BlockSpec((tm, width), lambda i: (i, 0)),
                   pl.BlockSpec((1, n_gates, tm), lambda i: (i // tiles_per_seq, 0, i % tiles_per_seq))],
        out_shape=[jax.ShapeDtypeStruct((n, width), BF16),
                   jax.ShapeDtypeStruct((n // seq_len, n_gates, seq_len), F32)],
        scratch_shapes=[pltpu.VMEM((3, CONV_PAD, GDN_HEADS * GDN_DK), F32)],
        compiler_params=_params("arbitrary"),
        name="gdn_in_proj",
    )(x, gain.reshape(1, d), w_main, conv_w, gate_w_t)


MLSTM_BLOCK = 512
MLSTM_CC = 128


def _mlstm_body(q_ref, k_ref, v_ref, og_ref, gate_ref, bias_ref, gain_ref, o_ref, c_scr, m_scr, ab_scr):
    n_heads = MLSTM_HEADS
    cc = MLSTM_CC
    block = q_ref.shape[1]

    @pl.when(pl.program_id(1) == 0)
    def _():
        c_scr[...] = jnp.zeros_like(c_scr)
        m_scr[...] = jnp.zeros_like(m_scr)

    pre = gate_ref[0] + bias_ref[...]
    log_i = GATE_SOFTCAP * jnp.tanh(pre[:n_heads] / GATE_SOFTCAP)
    log_f = -_softplus(-pre[n_heads:])
    b = _chunk_cumsum(log_f, cc)
    ab_scr[:n_heads] = log_i - b
    ab_scr[n_heads:] = b

    r_idx = lax.broadcasted_iota(jnp.int32, (cc, cc), 0)
    c_idx = lax.broadcasted_iota(jnp.int32, (cc, cc), 1)
    eye = r_idx == c_idx
    causal = c_idx <= r_idx
    ones_col = (lax.broadcasted_iota(jnp.int32, (cc, LANES), 1) == 0).astype(BF16)

    heads = range(n_heads)
    lanes = [slice(h * LANES, (h + 1) * LANES) for h in heads]
    m_prev = [m_scr[h:h + 1, 0:1] for h in heads]
    c_aug = [c_scr[h] for h in heads]
    for ci in range(block // cc):
        rows = slice(ci * cc, (ci + 1) * cc)
        q = [q_ref[0, rows, lanes[h]] for h in heads]
        k_t = [k_ref[0, rows, lanes[h]].astype(F32).T for h in heads]
        v_aug = [jnp.concatenate([v_ref[0, rows, lanes[h]], ones_col], axis=1) for h in heads]
        a_row = [ab_scr[h:h + 1, rows] for h in heads]
        b_col = [_col_from_row(ab_scr[n_heads + h:n_heads + h + 1, rows], eye) for h in heads]
        d = [jnp.where(causal, jnp.broadcast_to(a_row[h], (cc, cc)), -jnp.inf) for h in heads]
        g_col = [jnp.maximum(jnp.max(d[h], axis=1, keepdims=True), m_prev[h]) for h in heads]
        qk = [_dot(q[h], k_t[h].astype(BF16)) for h in heads]
        inter = [_dot(q[h], c_aug[h].astype(BF16)) for h in heads]
        s = [(qk[h] * jnp.exp(d[h] - g_col[h])).astype(BF16) for h in heads]
        g_last = [g_col[h][cc - 1:cc] for h in heads]
        kv = [_dot((k_t[h] * jnp.exp(a_row[h] - g_last[h])).astype(BF16), v_aug[h]) for h in heads]
        numden = [_dot(s[h], v_aug[h]) + jnp.exp(m_prev[h] - g_col[h]) * inter[h] for h in heads]
        c_aug = [jnp.exp(m_prev[h] - g_last[h]) * c_aug[h] + kv[h] for h in heads]
        m_prev = [b_col[h][cc - 1:cc] + g_last[h] for h in heads]
        for h in heads:
            num = numden[h][:, :MLSTM_DV]
            den = numden[h][:, MLSTM_DV:MLSTM_DV + 1]
            hh = num / jnp.maximum(jnp.abs(den), jnp.exp(-(b_col[h] + g_col[h])))
            hn = _rms(hh, gain_ref[:, lanes[h]]) * _sigmoid(og_ref[0, rows, lanes[h]])
            o_ref[0, rows, lanes[h]] = hn.astype(o_ref.dtype)
    for h in heads:
        c_scr[h] = c_aug[h]
        m_scr[h:h + 1, :] = jnp.broadcast_to(m_prev[h], (1, LANES))


def mlstm(proj, og, gates, bias, gain, q_blk, k_blk, v_blk):
    bsz, seq, _ = proj.shape
    width = MLSTM_HEADS * LANES
    blk = MLSTM_BLOCK

    def col(cb):
        return pl.BlockSpec((1, blk, width), lambda b, t: (b, t, cb))

    return pl.pallas_call(
        _mlstm_body,
        grid=(bsz, seq // blk),
        in_specs=[col(q_blk), col(k_blk), col(v_blk),
                  pl.BlockSpec((1, blk, width), lambda b, t: (b, t, 0)),
                  pl.BlockSpec((1, 2 * MLSTM_HEADS, blk), lambda b, t: (b, 0, t)),
                  pl.BlockSpec((2 * MLSTM_HEADS, 1), lambda b, t: (0, 0)),
                  pl.BlockSpec((1, width), lambda b, t: (0, 0))],
        out_specs=pl.BlockSpec((1, blk, width), lambda b, t: (b, t, 0)),
        out_shape=jax.ShapeDtypeStruct((bsz, seq, width), BF16),
        scratch_shapes=[pltpu.VMEM((MLSTM_HEADS, LANES, 2 * LANES), F32),
                        pltpu.VMEM((8, LANES), F32),
                        pltpu.VMEM((2 * MLSTM_HEADS, blk), F32)],
        compiler_params=_params("parallel", "arbitrary"),
        name="mlstm",
    )(proj, proj, proj, og, gates, bias, gain)


SB_TQ = 1024
SB_KT = 256
SIGN_BIT = 0x80000000
SB_DEAD_LOG2 = -150.0
SB_Q_SCALE = -(SB_DH ** -0.5) * 1.4426950408889634


def _sb_body(q_ref, k_ref, v_ref, o_ref, acc_scr, run_scr):
    tq = q_ref.shape[1]
    kt = SB_KT
    n_sub = tq // kt
    first_tile = pl.program_id(2) * n_sub
    acc_scr[...] = jnp.zeros_like(acc_scr)
    run_scr[...] = jnp.zeros_like(run_scr)

    suffix = (lax.broadcasted_iota(jnp.int32, (kt, kt), 0) > lax.broadcasted_iota(jnp.int32, (kt, kt), 1)).astype(BF16)

    row_idx = lax.broadcasted_iota(jnp.int32, (tq, kt), 0)
    col_idx = lax.broadcasted_iota(jnp.int32, (tq, kt), 1)

    def band(d, mask):
        tiles = [pl.ds(pl.multiple_of(jnp.maximum(first_tile + r - d, 0) * kt, kt), kt) for r in range(n_sub)]
        zn = jnp.concatenate([_dot_nt(q_ref[0, r * kt:(r + 1) * kt, :], k_ref[0, tiles[r], :])
                              for r in range(n_sub)], axis=0)
        neg_abs = lax.bitcast_convert_type(lax.bitcast_convert_type(zn, jnp.uint32) | jnp.uint32(SIGN_BIT), F32)
        sp = jnp.log2(1.0 + jnp.exp2(neg_abs))
        log_1m = jnp.minimum(zn, 0.0) - sp
        log_beta = log_1m - zn
        if mask is not None:
            log_1m = jnp.where(mask, log_1m, 0.0)
        run = run_scr[...]
        sums = _dot(log_1m.astype(BF16), suffix)
        w = jnp.exp2(log_beta + sums + jnp.concatenate([run] * (kt // LANES), axis=1))
        if mask is not None:
            w = jnp.where(mask, w, 0.0)
        w = w.astype(BF16)
        for r in range(n_sub):
            rows = slice(r * kt, (r + 1) * kt)
            acc_scr[rows, :] += _dot(w[rows], v_ref[0, tiles[r], :])
        run_scr[...] = run + jnp.sum(log_1m, axis=-1, keepdims=True)

    band(0, col_idx < row_idx % kt)

    def live(d, last):
        return jnp.logical_and(d <= last, jnp.max(run_scr[...]) > SB_DEAD_LOG2)

    def whole_band(d):
        band(d, None)
        return d + 1

    def ragged_band(d):
        band(d, row_idx // kt >= d - first_tile)
        return d + 1

    d = lax.while_loop(lambda d: live(d, first_tile), whole_band, 1)
    lax.while_loop(lambda d: live(d, first_tile + n_sub - 1), ragged_band, jnp.maximum(d, first_tile + 1))
    o_ref[0] = acc_scr[...].astype(o_ref.dtype)


def stick_breaking(proj, q_blk, k_blk, v_blk):
    bsz, seq, _ = proj.shape
    tq = SB_TQ
    return pl.pallas_call(
        _sb_body,
        grid=(bsz, SB_HEADS, seq // tq),
        in_specs=[pl.BlockSpec((1, tq, SB_DH), lambda b, h, i: (b, i, q_blk + h)),
                  pl.BlockSpec((1, seq, SB_DH), lambda b, h, i: (b, 0, k_blk + h)),
                  pl.BlockSpec((1, seq, SB_DH), lambda b, h, i: (b, 0, v_blk + h))],
        out_specs=pl.BlockSpec((1, tq, SB_DH), lambda b, h, i: (b, i, h)),
        out_shape=jax.ShapeDtypeStruct((bsz, seq, SB_HEADS * SB_DH), BF16),
        scratch_shapes=[pltpu.VMEM((tq, SB_DH), F32), pltpu.VMEM((tq, LANES), F32)],
        compiler_params=_params("parallel", "parallel", "arbitrary"),
        name="stickbreak",
    )(proj, proj, proj)


GDN_BLOCK = 512
GDN_CC = 128
GDN_GROUP = 8
GDN_BASE = 8
GDN_BASE_STEPS = 2


def _gdn_body(q_ref, k_ref, v_ref, og_ref, gate_ref, alog_ref, dtb_ref, gain_ref,
              o_ref, s_scr, u_scr, w_scr, att_scr, qd_scr, kdt_scr, dl_scr, oc_scr):
    block = q_ref.shape[1]
    cc = GDN_CC
    n_chunks = block // cc
    n_heads = GDN_HEADS
    group = GDN_GROUP

    @pl.when(pl.program_id(2) == 0)
    def _():
        s_scr[...] = jnp.zeros_like(s_scr)

    pre = gate_ref[0]
    beta_all = _sigmoid(pre[:n_heads])
    log_alpha = -jnp.exp(alog_ref[...]) * _softplus(pre[n_heads:] + dtb_ref[...])
    g_all = _chunk_cumsum(log_alpha, cc)

    r_idx = lax.broadcasted_iota(jnp.int32, (cc, cc), 0)
    c_idx = lax.broadcasted_iota(jnp.int32, (cc, cc), 1)
    eye = r_idx == c_idx
    incl = c_idx <= r_idx
    strict = c_idx < r_idx
    eye_f = eye.astype(F32)
    base_mask = (r_idx // GDN_BASE) == (c_idx // GDN_BASE)
    merge_masks = []
    size = GDN_BASE
    while size < cc:
        merge_masks.append(((r_idx // (2 * size)) == (c_idx // (2 * size))) & ((r_idx // size) != (c_idx // size)))
        size *= 2
    head_idx =lax.broadcasted_iota(jnp.int32, (n_heads, block), 0)

    problems = [(hh, c) for hh in range(group) for c in range(n_chunks)]
    head_gates = []
    for hh in range(group):
        is_head = head_idx == pl.program_id(1) * group + hh
        g_head = jnp.sum(jnp.where(is_head, g_all, 0.0), axis=0, keepdims=True)
        beta_head = jnp.sum(jnp.where(is_head, beta_all, 0.0), axis=0, keepdims=True)
        head_gates.append((g_head, beta_head))

    def l2n(x, scale):
        return x * (lax.rsqrt(jnp.sum(x * x, axis=-1, keepdims=True) + EPS) * scale)

    pre_state = []
    for hh, c in problems:
        rows = slice(c * cc, (c + 1) * cc)
        lanes = slice(hh * LANES, (hh + 1) * LANES)
        g_head, beta_head = head_gates[hh]
        q = l2n(q_ref[0, rows, lanes].astype(F32), GDN_DK ** -0.5)
        k = l2n(k_ref[0, rows, lanes].astype(F32), 1.0)
        q_b = q.astype(BF16)
        k_b = k.astype(BF16)
        v = v_ref[0, rows, lanes].astype(F32)
        g_row = g_head[:, rows]
        g_col = _col_from_row(g_row, eye)
        beta_col = _col_from_row(beta_head[:, rows], eye)
        decay = jnp.where(incl, jnp.exp(g_col - g_row), 0.0)
        kk = _dot_nt(k_b, k_b)
        qk = _dot_nt(q_b, k_b)
        exp_g = jnp.exp(g_col)
        g_last = g_col[cc - 1:cc]
        att_scr[hh, rows] = (qk * decay).astype(BF16)
        qd_scr[hh, rows] = (q * exp_g).astype(BF16)
        kdt_scr[hh, :, rows] = (k * jnp.exp(g_last - g_col)).T.astype(BF16)
        dl_scr[hh, c] = jnp.broadcast_to(jnp.exp(g_last), (8, LANES))
        rhs = jnp.concatenate([v * beta_col, k * (beta_col * exp_g)], axis=1)
        pre_state.append((jnp.where(strict, beta_col * kk * decay, 0.0), rhs))

    low = [m for m, _ in pre_state]
    powers = [jnp.where(base_mask, -m, 0.0) for m in low]
    n_accs = list(powers)
    for _ in range(GDN_BASE_STEPS):
        powers = [_dot(p.astype(BF16), p.astype(BF16)) for p in powers]
        n_accs = [n + p + _dot(n.astype(BF16), p.astype(BF16)) for n, p in zip(n_accs, powers)]
    invs = [eye_f + n for n in n_accs]
    for off_mask in merge_masks:
        inv_b = [t.astype(BF16) for t in invs]
        corr = [_dot(jnp.where(off_mask, m, 0.0).astype(BF16), t) for m, t in zip(low, inv_b)]
        invs = [t - _dot(t_b, e.astype(BF16)) for t, t_b, e in zip(invs, inv_b, corr)]
    for (hh, c), inv, (_, rhs) in zip(problems, invs, pre_state):
        rows = slice(c * cc, (c + 1) * cc)
        sol = rhs + _dot((inv - eye_f).astype(BF16), rhs.astype(BF16))
        u_scr[hh, rows] = sol[:, :GDN_DV]
        w_scr[hh, rows] = sol[:, GDN_DV:].astype(BF16)

    state = [s_scr[hh] for hh in range(group)]
    heads = range(group)
    for c in range(n_chunks):
        rows = slice(c * cc, (c + 1) * cc)
        s_b = [state[hh].astype(BF16) for hh in heads]
        v_b = [(u_scr[hh, rows] - _dot(w_scr[hh, rows], s_b[hh])).astype(BF16) for hh in heads]
        state = [state[hh] * dl_scr[hh, c, 0:1, :] + _dot(kdt_scr[hh, :, rows], v_b[hh]) for hh in heads]
        for hh in heads:
            oc_scr[rows, hh * LANES:(hh + 1) * LANES] = _dot(qd_scr[hh, rows], s_b[hh]) + _dot(att_scr[hh, rows], v_b[hh])
    for hh in heads:
        s_scr[hh] = state[hh]

    for hh in range(group):
        lanes = slice(hh * LANES, (hh + 1) * LANES)
        og = og_ref[0, :, lanes].astype(F32)
        o_ref[0, :, lanes] = (_rms(oc_scr[:, lanes], gain_ref[...]) * og).astype(o_ref.dtype)


def gated_deltanet(proj, gates, a_log, dt_bias, gain):
    bsz, seq, _ = proj.shape
    n_heads = GDN_HEADS
    blk = GDN_BLOCK

    grp = GDN_GROUP
    n_groups = n_heads // grp
    width = grp * LANES

    def col(part):
        return pl.BlockSpec((1, blk, width), lambda b, h, t: (b, t, part * n_groups + h))

    small = pl.BlockSpec((n_heads, 1), lambda b, h, t: (0, 0))
    return pl.pallas_call(
        _gdn_body,
        grid=(bsz, n_groups, seq // blk),
        in_specs=[col(0), col(1), col(2), col(3),
                  pl.BlockSpec((1, 2 * n_heads, blk), lambda b, h, t: (b, 0, t)),
                  small, small,
                  pl.BlockSpec((1, GDN_DV), lambda b, h, t: (0, 0))],
        out_specs=pl.BlockSpec((1, blk, width), lambda b, h, t: (b, t, h)),
        out_shape=jax.ShapeDtypeStruct((bsz, seq, n_heads * GDN_DV), BF16),
        scratch_shapes=[pltpu.VMEM((grp, GDN_DK, GDN_DV), F32),
                        pltpu.VMEM((grp, blk, GDN_DV), F32),
                        pltpu.VMEM((grp, blk, GDN_DK), BF16),
                        pltpu.VMEM((grp, blk, GDN_CC), BF16),
                        pltpu.VMEM((grp, blk, GDN_DK), BF16),
                        pltpu.VMEM((grp, GDN_DK, blk), BF16),
                        pltpu.VMEM((grp, blk // GDN_CC, 8, LANES), F32),
                        pltpu.VMEM((blk, width), F32)],
        compiler_params=_params("parallel", "parallel", "arbitrary"),
        name="gdn",
    )(proj, proj, proj, proj, gates, a_log, dt_bias, gain)


def _xattn_body(*refs, n_acts, n_heads):
    x_ref = refs[0]
    a_refs = refs[1:1 + n_acts]
    w_refs = refs[1 + n_acts:1 + 2 * n_acts]
    g_ref, wq_ref, k_ref, v_ref, wo_ref, o_ref = refs[1 + 2 * n_acts:]
    x = x_ref[...]
    for a_ref, w_ref in zip(a_refs, w_refs):
        x = x + _dot(a_ref[...], w_ref[...])
    d = x.shape[1]
    dh = d // n_heads
    h = _rms(x, g_ref[...]).astype(BF16)
    q = _dot(h, wq_ref[...]).astype(BF16)
    outs = []
    for hd in range(n_heads):
        lanes = slice(hd * dh, (hd + 1) * dh)
        s = _dot_nt(q[:, lanes], k_ref[0, :, lanes]) * (dh ** -0.5)
        e = jnp.exp(s - jnp.max(s, axis=-1, keepdims=True))
        p = e / jnp.sum(e, axis=-1, keepdims=True)
        outs.append(_dot(p.astype(BF16), v_ref[0, :, lanes]).astype(BF16))
    o_ref[...] = x + _dot(jnp.concatenate(outs, axis=1), wo_ref[...])


def mix_out_xattn(x, acts, w_outs, gain, wq, k, v, wo, seq_len, tm=1024):
    n, d = x.shape
    mem_len = k.shape[1]
    tiles_per_seq = seq_len // tm
    in_specs = [pl.BlockSpec((tm, d), lambda i: (i, 0))]
    in_specs += [pl.BlockSpec((tm, a.shape[1]), lambda i: (i, 0)) for a in acts]
    in_specs += [pl.BlockSpec(w.shape, lambda i: (0, 0)) for w in w_outs]
    in_specs += [pl.BlockSpec((1, d), lambda i: (0, 0)),
                 pl.BlockSpec((d, d), lambda i: (0, 0)),
                 pl.BlockSpec((1, mem_len, d), lambda i: (i // tiles_per_seq, 0, 0)),
                 pl.BlockSpec((1, mem_len, d), lambda i: (i // tiles_per_seq, 0, 0)),
                 pl.BlockSpec((d, d), lambda i: (0, 0))]
    return pl.pallas_call(
        functools.partial(_xattn_body, n_acts=len(acts), n_heads=XA_HEADS),
        grid=(n // tm,),
        in_specs=in_specs,
        out_specs=pl.BlockSpec((tm, d), lambda i: (i, 0)),
        out_shape=jax.ShapeDtypeStruct((n, d), F32),
        compiler_params=_params("parallel"),
        name="xattn",
    )(x, *acts, *w_outs, gain.reshape(1, d), wq, k, v, wo)


def _mlp_body(x_ref, g_ref, w1_ref, w2_ref, fg_ref, o_ref, h_scr, acc_scr, *, final_norm):
    f = pl.program_id(1)

    @pl.when(f == 0)
    def _():
        h_scr[...] = _rms(x_ref[...], g_ref[...]).astype(BF16)
        acc_scr[...] = jnp.zeros_like(acc_scr)

    a = jnp.maximum(_dot(h_scr[...], w1_ref[...]), 0.0)
    acc_scr[...] += _dot((a * a).astype(BF16), w2_ref[...])

    @pl.when(f == pl.num_programs(1) - 1)
    def _():
        y = x_ref[...] + acc_scr[...]
        o_ref[...] = _rms(y, fg_ref[...]) if final_norm else y


def mlp(x, gain, w1, w2, final_gain, final_norm, tm=1024, tf=1024):
    n, d = x.shape
    d_ff = w1.shape[1]
    return pl.pallas_call(
        functools.partial(_mlp_body, final_norm=final_norm),
        grid=(n // tm, d_ff // tf),
        in_specs=[pl.BlockSpec((tm, d), lambda i, f: (i, 0)),
                  pl.BlockSpec((1, d), lambda i, f: (0, 0)),
                  pl.BlockSpec((d, tf), lambda i, f: (0, f)),
                  pl.BlockSpec((tf, d), lambda i, f: (f, 0)),
                  pl.BlockSpec((1, d), lambda i, f: (0, 0))],
        out_specs=pl.BlockSpec((tm, d), lambda i, f: (i, 0)),
        out_shape=jax.ShapeDtypeStruct((n, d), F32),
        scratch_shapes=[pltpu.VMEM((tm, d), BF16), pltpu.VMEM((tm, d), F32)],
        compiler_params=_params("parallel", "arbitrary"),
        name="mlp",
    )(x, gain.reshape(1, d), w1, w2, final_gain.reshape(1, d))


def _pad_heads(w, n_heads, dh):
    d = w.shape[0]
    w = w.reshape(d, n_heads, dh)
    return jnp.pad(w, ((0, 0), (0, 0), (0, LANES - dh))).reshape(d, n_heads * LANES)


def _even_layer(x2, seq_len, gain, w_in, b_i, b_f, head_gain, w_out):
    ml_qk = MLSTM_HEADS * MLSTM_DQK
    ml_v = MLSTM_HEADS * MLSTM_DV
    sb_w = SB_HEADS * SB_DH
    sizes = (ml_qk, ml_qk, ml_v, ml_v, MLSTM_HEADS, MLSTM_HEADS, sb_w, sb_w, sb_w)
    offs = [0]
    for s in sizes:
        offs.append(offs[-1] + s)
    part = [w_in[:, offs[i]:offs[i + 1]] for i in range(len(sizes))]
    mq, mk, mv, mo, mi, mf, sq, sk, sv = part
    w_main = jnp.concatenate([
        _pad_heads(mq, MLSTM_HEADS, MLSTM_DQK),
        _pad_heads(mk * (MLSTM_DQK ** -0.5), MLSTM_HEADS, MLSTM_DQK),
        mv, sq * SB_Q_SCALE, sk, sv], axis=1).astype(BF16)
    w_gate_t = jnp.concatenate([mi, mf], axis=1).T.astype(BF16)
    proj, og, gates = norm_proj(x2, gain, [w_main, mo.astype(BF16)], [BF16, F32], gate_w_t=w_gate_t, seq_len=seq_len)
    bsz = x2.shape[0] // seq_len
    proj3 = proj.reshape(bsz, seq_len, proj.shape[1])
    bias = jnp.concatenate([b_i, b_f]).reshape(2 * MLSTM_HEADS, 1)
    hm = mlstm(proj3, og.reshape(bsz, seq_len, ml_v), gates, bias, head_gain.reshape(1, ml_v), 0, 1, 2)
    hs = stick_breaking(proj3, 3 * SB_HEADS, 4 * SB_HEADS, 5 * SB_HEADS)
    w_out = w_out.astype(BF16)
    return [hm.reshape(-1, ml_v), hs.reshape(-1, sb_w)], [w_out[:ml_v], w_out[ml_v:]]


def _odd_layer(x2, seq_len, gain, w_in, conv_w, a_log, dt_bias, head_gain, w_out):
    conv_ch = 3 * GDN_HEADS * GDN_DK
    vw = GDN_HEADS * GDN_DV
    w_main = w_in[:, :conv_ch + vw].astype(BF16)
    w_gate_t = w_in[:, conv_ch + vw:].T.astype(BF16)
    proj, gates = gdn_in_proj(x2, gain, w_main, conv_w, w_gate_t, seq_len)
    bsz = x2.shape[0] // seq_len
    o = gated_deltanet(proj.reshape(bsz, seq_len, conv_ch + vw), gates,
                       a_log.reshape(GDN_HEADS, 1), dt_bias.reshape(GDN_HEADS, 1), head_gain.reshape(1, GDN_DV))
    return [o.reshape(-1, vw)], [w_out.astype(BF16)]


def kernel(x, mem, mix_norm, ab_w_in, ab_b_i, ab_b_f, ab_head_gain, ab_w_out, c_w_in, c_conv_w, c_a_log, c_dt_bias, c_head_gain, c_w_out, xa_norm, mem_norm, xa_wq, xa_wk, xa_wv, xa_wo, mlp_norm, mlp_w1, mlp_w2, final_norm):
    bsz, seq_len, d = x.shape
    mem_len = mem.shape[1]
    depth = mix_norm.shape[0]
    x2 = x.reshape(bsz * seq_len, d)
    mem2 = mem.reshape(bsz * mem_len, d)
    for l in range(depth):
        j = l // 2
        if l % 2 == 0:
            acts, w_outs = _even_layer(x2, seq_len, mix_norm[l], ab_w_in[j], ab_b_i[j], ab_b_f[j], ab_head_gain[j], ab_w_out[j])
        else:
            acts, w_outs = _odd_layer(x2, seq_len, mix_norm[l], c_w_in[j], c_conv_w[j], c_a_log[j], c_dt_bias[j], c_head_gain[j], c_w_out[j])
        mem_k, mem_v = norm_proj(mem2, mem_norm, [xa_wk[l].astype(BF16), xa_wv[l].astype(BF16)], [BF16, BF16], tm=mem_len)
        x2 = mix_out_xattn(x2, acts, w_outs, xa_norm[l], xa_wq[l].astype(BF16), mem_k.reshape(bsz, mem_len, d),
                           mem_v.reshape(bsz, mem_len, d), xa_wo[l].astype(BF16), seq_len)
        x2 = mlp(x2, mlp_norm[l], mlp_w1[l].astype(BF16), mlp_w2[l].astype(BF16), final_norm, l == depth - 1)
    return x2.reshape(bsz, seq_len, d)
```
